```python
import math
import jax, jax.numpy as jnp
from jax import lax
import numpy as np

D_MODEL = 1024
BATCH = 4
SEQ = 8192
DEPTH = 1

GLA_HEADS = 4
GLA_DK = D_MODEL // 2 // GLA_HEADS
GLA_DV = D_MODEL // GLA_HEADS
GLA_GATE_RANK = 16
GLA_TAU = 16.0
GLA_CHUNK = 64
MLA_HEADS = 8
MLA_NOPE = 128
MLA_ROPE = 64
MLA_V = 128
MLA_Q_RANK = 384
MLA_KV_RANK = 256
ROPE_THETA = 10000.0
Q_BLOCK = 128
N_BRANCHES = 2
N_EXPERTS = 16
EC_CAPACITY = 2
EXPERT_FF = 1024
EPS = 1e-6

IN_SPLITS = (
    GLA_HEADS * GLA_DK,
    GLA_HEADS * GLA_DK,
    GLA_HEADS * GLA_DV,
    GLA_HEADS * GLA_DV,
    GLA_GATE_RANK,
    GLA_GATE_RANK,
    MLA_Q_RANK,
    MLA_KV_RANK,
    MLA_ROPE,
    N_BRANCHES * D_MODEL,
)
IN_WIDTH = sum(IN_SPLITS)

kernel_name = "hybrid_gla_mla_expert_choice_block"


def rmsnorm(x, g):
    x32 = x.astype(jnp.float32)
    y = x32 * lax.rsqrt(jnp.mean(x32 * x32, axis=-1, keepdims=True) + EPS)
    return (y * g.astype(jnp.float32)).astype(x.dtype)


def rope(x, pos):
    half = x.shape[-1] // 2
    freqs = ROPE_THETA ** (-jnp.arange(half, dtype=jnp.float32) / half)
    ang = pos.astype(jnp.float32)[..., None] * freqs
    cos = jnp.cos(ang)[:, :, None, :]
    sin = jnp.sin(ang)[:, :, None, :]
    x32 = x.astype(jnp.float32)
    x1, x2 = x32[..., :half], x32[..., half:]
    return jnp.concatenate([x1 * cos - x2 * sin, x2 * cos + x1 * sin], axis=-1).astype(x.dtype)


def gla_direction(q, k, v, log_a, inclusive):
    B, H, S, dk = q.shape
    dv = v.shape[-1]
    nc = S // GLA_CHUNK
    q = q.reshape(B, H, nc, GLA_CHUNK, dk)
    k = k.reshape(B, H, nc, GLA_CHUNK, dk)
    v = v.reshape(B, H, nc, GLA_CHUNK, dv)
    b = jnp.cumsum(log_a.reshape(B, H, nc, GLA_CHUNK, dk), axis=3)
    q_t = q * jnp.exp(b)
    k_t = k * jnp.exp(-b)
    mask = jnp.tril(jnp.ones((GLA_CHUNK, GLA_CHUNK), bool), k=0 if inclusive else -1)
    att = jnp.where(mask, jnp.einsum('bhncd,bhnmd->bhncm', q_t, k_t), 0.0)
    o_intra = jnp.einsum('bhncm,bhnme->bhnce', att, v)
    b_last = b[..., -1:, :]
    k_dec = k * jnp.exp(b_last - b)
    d_state = jnp.einsum('bhncd,bhnce->nbhde', k_dec, v)
    decay = jnp.moveaxis(jnp.exp(b_last[..., 0, :]), 2, 0)

    def step(state, inp):
        dec, ds = inp
        return dec[..., None] * state + ds, state

    init = jnp.zeros((B, H, dk, dv), jnp.float32)
    _, s_prev = lax.scan(step, init, (decay, d_state))
    o_inter = jnp.einsum('bhncd,nbhde->bhnce', q_t, s_prev)
    return (o_intra + o_inter).reshape(B, H, S, dv)


def setup_inputs(seed: int = 0) -> dict:
    key = jax.random.key(seed)
    ks = jax.random.split(key, 24)

    def w(k, shape, fan_in):
        return jax.random.normal(k, shape, jnp.float32) * (fan_in ** -0.5)

    def gain(k, n):
        return 1.0 + 0.02 * jax.random.normal(k, (n,), jnp.float32)

    hq = MLA_NOPE + MLA_ROPE
    return {
        "x": jax.random.normal(ks[0], (BATCH, SEQ, D_MODEL), jnp.float32),
        "positions": jnp.tile(jnp.arange(SEQ, dtype=jnp.int32)[None, :], (BATCH, 1)),
        "g_mix": gain(ks[1], D_MODEL),
        "w_in": w(ks[2], (D_MODEL, IN_WIDTH), D_MODEL),
        "gla_wa2_f": w(ks[3], (GLA_GATE_RANK, GLA_HEADS * GLA_DK), GLA_GATE_RANK),
        "gla_ba_f": 0.1 * jax.random.normal(ks[4], (GLA_HEADS * GLA_DK,), jnp.float32),
        "gla_wa2_b": w(ks[5], (GLA_GATE_RANK, GLA_HEADS * GLA_DK), GLA_GATE_RANK),
        "gla_ba_b": 0.1 * jax.random.normal(ks[6], (GLA_HEADS * GLA_DK,), jnp.float32),
        "gla_onorm": gain(ks[7], GLA_DV),
        "w_gla_out": w(ks[8], (GLA_HEADS * GLA_DV, D_MODEL), GLA_HEADS * GLA_DV),
        "mla_q_norm": gain(ks[9], MLA_Q_RANK),
        "w_uq": w(ks[10], (MLA_Q_RANK, MLA_HEADS * hq), MLA_Q_RANK),
        "mla_kv_norm": gain(ks[11], MLA_KV_RANK),
        "w_ukv": w(ks[12], (MLA_KV_RANK, MLA_HEADS * (MLA_NOPE + MLA_V)), MLA_KV_RANK),
        "q_head_norm": gain(ks[13], hq),
        "k_head_norm": gain(ks[14], hq),
        "w_mla_out": w(ks[15], (MLA_HEADS * MLA_V, D_MODEL), MLA_HEADS * MLA_V),
        "w_o": w(ks[16], (D_MODEL, D_MODEL), D_MODEL),
        "g_ffn": gain(ks[17], D_MODEL),
        "w_router": w(ks[18], (D_MODEL, N_EXPERTS), D_MODEL),
        "w_gate_e": w(ks[19], (N_EXPERTS, D_MODEL, EXPERT_FF), D_MODEL),
        "w_up_e": w(ks[20], (N_EXPERTS, D_MODEL, EXPERT_FF), D_MODEL),
        "w_down_e": w(ks[21], (N_EXPERTS, EXPERT_FF, D_MODEL), EXPERT_FF),
    }


def token_mixer(x, positions, g_mix, w_in, gla_wa2_f, gla_ba_f, gla_wa2_b, gla_ba_b,
                gla_onorm, w_gla_out, mla_q_norm, w_uq, mla_kv_norm, w_ukv,
                q_head_norm, k_head_norm, w_mla_out, w_o):
    B, S, D = x.shape
    h = rmsnorm(x, g_mix)
    proj = h @ w_in
    offs = [int(o) for o in np.cumsum(IN_SPLITS)[:-1]]
    (gq, gk, gv, gr, gaf, gab, cq, ckv, kr, gates) = jnp.split(proj, offs, axis=-1)

    def heads(t, d):
        return t.reshape(B, S, GLA_HEADS, d).transpose(0, 2, 1, 3).astype(jnp.float32)

    q_a = heads(gq, GLA_DK) * (GLA_DK ** -0.5)
    k_a = heads(gk, GLA_DK)
    v_a = heads(gv, GLA_DV)
    la_f = heads(jax.nn.log_sigmoid(gaf @ gla_wa2_f + gla_ba_f), GLA_DK) / GLA_TAU
    la_b = heads(jax.nn.log_sigmoid(gab @ gla_wa2_b + gla_ba_b), GLA_DK) / GLA_TAU
    o_f = gla_direction(q_a, k_a, v_a, la_f, inclusive=True)
    flip = lambda t: jnp.flip(t, axis=2)
    o_b = flip(gla_direction(flip(q_a), flip(k_a), flip(v_a), flip(la_b), inclusive=False))
    o_a = (o_f + o_b).transpose(0, 2, 1, 3).astype(x.dtype)
    o_a = rmsnorm(o_a, gla_onorm) * jax.nn.silu(gr.reshape(B, S, GLA_HEADS, GLA_DV))
    y_a = o_a.reshape(B, S, GLA_HEADS * GLA_DV) @ w_gla_out

    hq = MLA_NOPE + MLA_ROPE
    q_b = (rmsnorm(cq, mla_q_norm) @ w_uq).reshape(B, S, MLA_HEADS, hq)
    kv = (rmsnorm(ckv, mla_kv_norm) @ w_ukv).reshape(B, S, MLA_HEADS, MLA_NOPE + MLA_V)
    k_nope, v_b = kv[..., :MLA_NOPE], kv[..., MLA_NOPE:]
    k_rope = jnp.broadcast_to(kr[:, :, None, :], (B, S, MLA_HEADS, MLA_ROPE))
    k_b = jnp.concatenate([k_nope, k_rope], axis=-1)
    q_b = rmsnorm(q_b, q_head_norm)
    k_b = rmsnorm(k_b, k_head_norm)
    q_b = jnp.concatenate([q_b[..., :MLA_NOPE], rope(q_b[..., MLA_NOPE:], positions)], axis=-1)
    k_b = jnp.concatenate([k_b[..., :MLA_NOPE], rope(k_b[..., MLA_NOPE:], positions)], axis=-1)
    k_t = k_b.transpose(0, 2, 1, 3)
    v_t = v_b.transpose(0, 2, 1, 3)
    nb = S // Q_BLOCK
    q_blocks = q_b.reshape(B, nb, Q_BLOCK, MLA_HEADS, hq).transpose(1, 0, 3, 2, 4)
    scale = 1.0 / math.sqrt(hq)

    def attend(qblk):
        s = jnp.einsum('bhqd,bhkd->bhqk', qblk, k_t).astype(jnp.float32) * scale
        p = jax.nn.softmax(s, axis=-1)
        return jnp.einsum('bhqk,bhkd->bhqd', p.astype(v_t.dtype), v_t)

    o_b = lax.map(attend, q_blocks)
    o_b = o_b.transpose(1, 0, 3, 2, 4).reshape(B, S, MLA_HEADS * MLA_V)
    y_b = o_b @ w_mla_out

    g = jax.nn.sigmoid(gates.astype(jnp.float32)).astype(x.dtype)
    merged = g[..., :D] * y_a + g[..., D:] * y_b
    return merged @ w_o


def expert_choice_ffn(x, g_ffn, w_router, w_gate_e, w_up_e, w_down_e):
    B, S, D = x.shape
    cap = EC_CAPACITY * S // N_EXPERTS
    h = rmsnorm(x, g_ffn)
    aff = jax.nn.softmax((h @ w_router).astype(jnp.float32), axis=-1)
    vals, idx = lax.top_k(aff.transpose(0, 2, 1), cap)
    xe = jax.vmap(lambda hb, ib: hb[ib])(h, idx)
    a = jnp.einsum('becd,edf->becf', xe, w_gate_e)
    u = jnp.einsum('becd,edf->becf', xe, w_up_e)
    ye = jnp.einsum('becf,efd->becd', jax.nn.silu(a) * u, w_down_e)
    ye = ye * vals[..., None].astype(ye.dtype)
    flat = (jnp.arange(B, dtype=jnp.int32)[:, None, None] * S + idx).reshape(-1)
    out = jax.ops.segment_sum(ye.reshape(-1, D), flat, num_segments=B * S)
    return out.reshape(B, S, D).astype(x.dtype)


def reference(x, positions, g_mix, w_in, gla_wa2_f, gla_ba_f, gla_wa2_b, gla_ba_b,
              gla_onorm, w_gla_out, mla_q_norm, w_uq, mla_kv_norm, w_ukv,
              q_head_norm, k_head_norm, w_mla_out, w_o, g_ffn, w_router,
              w_gate_e, w_up_e, w_down_e):
    for _ in range(DEPTH):
        x = x + token_mixer(x, positions, g_mix, w_in, gla_wa2_f, gla_ba_f, gla_wa2_b,
                            gla_ba_b, gla_onorm, w_gla_out, mla_q_norm, w_uq,
                            mla_kv_norm, w_ukv, q_head_norm, k_head_norm, w_mla_out, w_o)
        x = x + expert_choice_ffn(x, g_ffn, w_router, w_gate_e, w_up_e, w_down_e)
    return x
```

```python
import functools
import math

import jax
import jax.numpy as jnp
from jax import lax
from jax.experimental import pallas as pl
from jax.experimental.pallas import tpu as pltpu

D_MODEL = 1024
GLA_HEADS = 4
GLA_DK = 128
GLA_DV = 256
GLA_GATE_RANK = 16
GLA_TAU = 16.0
GLA_CHUNK = 64
MLA_HEADS = 8
MLA_NOPE = 128
MLA_ROPE = 64
MLA_V = 128
MLA_Q_RANK = 384
MLA_KV_RANK = 256
MLA_HQ = MLA_NOPE + MLA_ROPE
ROPE_THETA = 10000.0
N_EXPERTS = 16
EC_CAPACITY = 2
EXPERT_FF = 1024
EPS = 1e-6

LANES = 128
VMEM_LIMIT = 52 * 1024 * 1024

BF16 = jnp.bfloat16
F32 = jnp.float32


def _dot(a, b):
    return jnp.dot(a, b, preferred_element_type=F32)


def _dot_nt(a, b):
    return lax.dot_general(a, b, (((1,), (1,)), ((), ())), preferred_element_type=F32)


def _dot_tn(a, b):
    return lax.dot_general(a, b, (((0,), (0,)), ((), ())), preferred_element_type=F32)


def _rms(x, gain):
    ms = jnp.mean(x * x, axis=-1, keepdims=True)
    return x * lax.rsqrt(ms + EPS) * gain


def _const_spec(shape):
    nd = len(shape)
    return pl.BlockSpec(shape, lambda *_: (0,) * nd)


def _params(sem):
    return pltpu.CompilerParams(dimension_semantics=sem, vmem_limit_bytes=VMEM_LIMIT)


def _gla_in_kernel(x_ref, g_ref, wmain_ref, wdec_ref, wa2_ref, ba_ref, wgate_ref,
                   q_ref, k_ref, v_ref, r_ref, la_ref, gate_ref):
    h = _rms(x_ref[...], g_ref[...]).astype(BF16)
    nqk = GLA_HEADS * GLA_DK
    nv = GLA_HEADS * GLA_DV
    q_ref[...] = (_dot(h, wmain_ref[:, 0:nqk]) * (GLA_DK ** -0.5)).astype(BF16)
    k_ref[...] = _dot(h, wmain_ref[:, nqk:2 * nqk]).astype(BF16)
    v_ref[...] = _dot(h, wmain_ref[:, 2 * nqk:2 * nqk + nv]).astype(BF16)
    r_ref[...] = _dot(h, wmain_ref[:, 2 * nqk + nv:2 * nqk + 2 * nv]).astype(BF16)
    dec = _dot(h, wdec_ref[...]).astype(BF16)
    z = _dot(dec, wa2_ref[...]) + ba_ref[...]
    log_sig = jnp.minimum(z, 0.0) - jnp.log1p(jnp.exp(-jnp.abs(z)))
    la_ref[...] = log_sig * (1.0 / GLA_TAU)
    gate_ref[...] = jax.nn.sigmoid(_dot(h, wgate_ref[...])).astype(BF16)


def _gla_in(x2, g_mix, w_main, w_dec, wa2, ba, w_gates, tm):
    T = x2.shape[0]
    nqk = GLA_HEADS * GLA_DK
    nv = GLA_HEADS * GLA_DV
    row = lambda n: pl.BlockSpec((tm, n), lambda i: (i, 0))
    return pl.pallas_call(
        _gla_in_kernel,
        grid=(T // tm,),
        in_specs=[row(D_MODEL), _const_spec(g_mix.shape), _const_spec(w_main.shape),
                  _const_spec(w_dec.shape), _const_spec(wa2.shape), _const_spec(ba.shape),
                  _const_spec(w_gates.shape)],
        out_specs=[row(nqk), row(nqk), row(nv), row(nv), row(2 * nqk), row(2 * D_MODEL)],
        out_shape=[jax.ShapeDtypeStruct((T, nqk), BF16), jax.ShapeDtypeStruct((T, nqk), BF16),
                   jax.ShapeDtypeStruct((T, nv), BF16), jax.ShapeDtypeStruct((T, nv), BF16),
                   jax.ShapeDtypeStruct((T, 2 * nqk), F32),
                   jax.ShapeDtypeStruct((T, 2 * D_MODEL), BF16)],
        compiler_params=_params(("parallel",)),
        name="gla_in",
    )(x2, g_mix, w_main, w_dec, wa2, ba, w_gates)


def _rope128(x, cos, sin_signed):
    lane = lax.broadcasted_iota(jnp.int32, x.shape, 1)
    first_half = (lane % MLA_ROPE) < (MLA_ROPE // 2)
    partner = jnp.where(first_half,
                        pltpu.roll(x, LANES - MLA_ROPE // 2, 1),
                        pltpu.roll(x, MLA_ROPE // 2, 1))
    return x * cos + partner * sin_signed


def _mla_in_kernel(x_ref, g_ref, pos_ref, freq_ref, sign_ref, wmla_ref, qn_ref, wuq_ref,
                   kvn_ref, wuk_ref, wvt_ref, qhn_ref, qhr_ref, khn_ref, khr_ref,
                   q_ref, k_ref, vt_ref, *, q_scale):
    h = _rms(x_ref[...], g_ref[...]).astype(BF16)
    lat = _dot(h, wmla_ref[...])
    cq = lat[:, 0:MLA_Q_RANK]
    ckv = lat[:, MLA_Q_RANK:MLA_Q_RANK + MLA_KV_RANK]
    kr2 = lat[:, MLA_Q_RANK + MLA_KV_RANK:MLA_Q_RANK + MLA_KV_RANK + LANES]

    ang = pos_ref[...] * freq_ref[...]
    cos = jnp.cos(ang)
    sin_signed = jnp.sin(ang) * sign_ref[...]
    lane = lax.broadcasted_iota(jnp.int32, ang.shape, 1)
    low = lane < MLA_ROPE

    nn = MLA_HEADS * MLA_NOPE
    qf = _dot(_rms(cq, qn_ref[...]).astype(BF16), wuq_ref[...])
    ckvn = _rms(ckv, kvn_ref[...]).astype(BF16)
    kn = _dot(ckvn, wuk_ref[...])
    vt_ref[...] = _dot_nt(wvt_ref[...], ckvn).astype(BF16)

    ss_kr = jnp.sum(kr2 * kr2, axis=-1, keepdims=True)
    kr_rot = _rope128(kr2 * khr_ref[...], cos, sin_signed)[:, 0:MLA_ROPE]

    for pair in range(MLA_HEADS // 2):
        xr = qf[:, nn + pair * LANES: nn + (pair + 1) * LANES]
        sq = xr * xr
        ss_lo = jnp.sum(jnp.where(low, sq, 0.0), axis=-1, keepdims=True)
        ss_hi = jnp.sum(sq, axis=-1, keepdims=True) - ss_lo
        rstd = []
        for j, ss_r in enumerate((ss_lo, ss_hi)):
            hd = 2 * pair + j
            xn = qf[:, hd * MLA_NOPE:(hd + 1) * MLA_NOPE]
            ss = jnp.sum(xn * xn, axis=-1, keepdims=True) + ss_r
            r = lax.rsqrt(ss * (1.0 / MLA_HQ) + EPS)
            rstd.append(r)
            q_ref[hd, :, 0:MLA_NOPE] = (xn * r * (qhn_ref[...] * q_scale)).astype(BF16)
        scaled = xr * jnp.where(low, rstd[0], rstd[1]) * (qhr_ref[...] * q_scale)
        rot = _rope128(scaled, cos, sin_signed).astype(BF16)
        q_ref[2 * pair, :, MLA_NOPE:MLA_HQ] = rot[:, 0:MLA_ROPE]
        q_ref[2 * pair + 1, :, MLA_NOPE:MLA_HQ] = rot[:, MLA_ROPE:LANES]

    for hd in range(MLA_HEADS):
        xn = kn[:, hd * MLA_NOPE:(hd + 1) * MLA_NOPE]
        ss = jnp.sum(xn * xn, axis=-1, keepdims=True) + ss_kr
        r = lax.rsqrt(ss * (1.0 / MLA_HQ) + EPS)
        k_ref[hd, :, 0:MLA_NOPE] = (xn * r * khn_ref[...]).astype(BF16)
        k_ref[hd, :, MLA_NOPE:MLA_HQ] = (kr_rot * r).astype(BF16)


def _mla_in(x2, g_mix, pos, freq, sign, w_mla, qn, w_uq, kvn, w_uk, w_vt, qhn, qhr, khn, khr,
            B, S, tm, q_scale):
    nsb = S // tm
    consts = (g_mix, None, freq, sign, w_mla, qn, w_uq, kvn, w_uk, w_vt, qhn, qhr, khn, khr)
    in_specs = [pl.BlockSpec((tm, D_MODEL), lambda i: (i, 0))]
    for c in consts:
        in_specs.append(pl.BlockSpec((tm, 1), lambda i: (i, 0)) if c is None else _const_spec(c.shape))
    head_spec = pl.BlockSpec((None, MLA_HEADS, tm, MLA_HQ), lambda i: (i // nsb, 0, i % nsb, 0))
    return pl.pallas_call(
        functools.partial(_mla_in_kernel, q_scale=q_scale),
        grid=(B * nsb,),
        in_specs=in_specs,
        out_specs=[head_spec, head_spec,
                   pl.BlockSpec((None, MLA_HEADS * MLA_V, tm), lambda i: (i // nsb, 0, i % nsb))],
        out_shape=[jax.ShapeDtypeStruct((B, MLA_HEADS, S, MLA_HQ), BF16),
                   jax.ShapeDtypeStruct((B, MLA_HEADS, S, MLA_HQ), BF16),
                   jax.ShapeDtypeStruct((B, MLA_HEADS * MLA_V, S), BF16)],
        compiler_params=_params(("parallel",)),
        name="mla_in",
    )(x2, g_mix, pos, freq, sign, w_mla, qn, w_uq, kvn, w_uk, w_vt, qhn, qhr, khn, khr)


def _gla_block(q_ref, k_ref, v_ref, la_ref, st_ref, emit, *, reverse, la_off, n_chunks):
    C = GLA_CHUNK
    row = lax.broadcasted_iota(jnp.int32, (C, C), 0)
    col = lax.broadcasted_iota(jnp.int32, (C, C), 1)
    if reverse:
        cum_mat = (col >= row).astype(BF16)
        att_mask = col > row
        last = 0
    else:
        cum_mat = (col <= row).astype(BF16)
        att_mask = col <= row
        last = C - 1
    order = range(n_chunks - 1, -1, -1) if reverse else range(n_chunks)
    for c in order:
        rows = slice(c * C, (c + 1) * C)
        la = la_ref[rows, la_off:la_off + GLA_HEADS * GLA_DK]
        la_hi = la.astype(BF16)
        la_lo = (la - la_hi.astype(F32)).astype(BF16)
        b_all = _dot(cum_mat, la_hi) + _dot(cum_mat, la_lo)
        for hd in range(GLA_HEADS):
            kcols = slice(hd * GLA_DK, (hd + 1) * GLA_DK)
            vcols = slice(hd * GLA_DV, (hd + 1) * GLA_DV)
            b = b_all[:, kcols]
            b_last = b[last:last + 1, :]
            q = q_ref[rows, kcols].astype(F32)
            k = k_ref[rows, kcols].astype(F32)
            v = v_ref[rows, vcols]
            q_t = (q * jnp.exp(b)).astype(BF16)
            k_t = (k * jnp.exp(-b)).astype(BF16)
            k_dec = (k * jnp.exp(b_last - b)).astype(BF16)
            att = jnp.where(att_mask, _dot_nt(q_t, k_t), 0.0).astype(BF16)
            st = st_ref[hd]
            o = _dot(att, v) + _dot_nt(q_t, st.astype(BF16))
            st_ref[hd] = st * jnp.exp(b_last) + _dot_tn(v, k_dec)
            emit(rows, vcols, o)


def _gla_fwd_kernel(q_ref, k_ref, v_ref, la_ref, o_ref, st_ref, *, n_chunks):
    @pl.when(pl.program_id(1) == 0)
    def _():
        st_ref[...] = jnp.zeros_like(st_ref)

    def emit(rows, vcols, o):
        o_ref[rows, vcols] = o

    _gla_block(q_ref, k_ref, v_ref, la_ref, st_ref, emit, reverse=False, la_off=0, n_chunks=n_chunks)


def _gla_bwd_kernel(q_ref, k_ref, v_ref, la_ref, of_ref, r_ref, gn_ref, o_ref, st_ref, *, n_chunks):
    @pl.when(pl.program_id(1) == 0)
    def _():
        st_ref[...] = jnp.zeros_like(st_ref)

    def emit(rows, vcols, o):
        tot = of_ref[rows, vcols] + o
        gate = r_ref[rows, vcols].astype(F32)
        o_ref[rows, vcols] = (_rms(tot, gn_ref[...]) * (gate * jax.nn.sigmoid(gate))).astype(BF16)

    _gla_block(q_ref, k_ref, v_ref, la_ref, st_ref, emit, reverse=True,
               la_off=GLA_HEADS * GLA_DK, n_chunks=n_chunks)


def _gla(gq, gk, gv, la, gr, gnorm, B, S, tb):
    nb = S // tb
    nqk = GLA_HEADS * GLA_DK
    nv = GLA_HEADS * GLA_DV
    n_chunks = tb // GLA_CHUNK
    state = pltpu.VMEM((GLA_HEADS, GLA_DV, GLA_DK), F32)
    fwd = lambda n: pl.BlockSpec((tb, n), lambda b, i: (b * nb + i, 0))
    rev = lambda n: pl.BlockSpec((tb, n), lambda b, i: (b * nb + nb - 1 - i, 0))
    o_f = pl.pallas_call(
        functools.partial(_gla_fwd_kernel, n_chunks=n_chunks),
        grid=(B, nb),
        in_specs=[fwd(nqk), fwd(nqk), fwd(nv), fwd(2 * nqk)],
        out_specs=fwd(nv),
        out_shape=jax.ShapeDtypeStruct((B * S, nv), F32),
        scratch_shapes=[state],
        compiler_params=_params(("parallel", "arbitrary")),
        name="gla_fwd",
    )(gq, gk, gv, la)
    return pl.pallas_call(
        functools.partial(_gla_bwd_kernel, n_chunks=n_chunks),
        grid=(B, nb),
        in_specs=[rev(nqk), rev(nqk), rev(nv), rev(2 * nqk), rev(nv), rev(nv),
                  pl.BlockSpec(gnorm.shape, lambda b, i: (0, 0))],
        out_specs=rev(nv),
        out_shape=jax.ShapeDtypeStruct((B * S, nv), BF16),
        scratch_shapes=[state],
        compiler_params=_params(("parallel", "arbitrary")),
        name="gla_bwd",
    )(gq, gk, gv, la, o_f, gr, gnorm)


def _attn_kernel(q_ref, k_ref, vt_ref, o_ref, acc_ref, *, tk, n_kv):
    q = q_ref[...]
    tq = q.shape[0]
    acc_ref[...] = jnp.zeros_like(acc_ref)

    def body(j, carry):
        m, l = carry
        off = pl.multiple_of(j * tk, tk)
        s = _dot_nt(k_ref[pl.ds(off, tk), :], q)
        m_new = jnp.maximum(m, jnp.max(s, axis=0, keepdims=True))
        alpha = jnp.exp2(m - m_new)
        p = jnp.exp2(s - m_new)
        l_new = alpha * l + jnp.sum(p, axis=0, keepdims=True)
        acc_ref[...] = alpha * acc_ref[...] + _dot(vt_ref[:, pl.ds(off, tk)], p.astype(BF16))
        return m_new, l_new

    m0 = jnp.full((1, tq), -1e30, F32)
    l0 = jnp.zeros((1, tq), F32)
    _, l = lax.fori_loop(0, n_kv, body, (m0, l0))
    o_ref[...] = (acc_ref[...] * (1.0 / l)).T.astype(BF16)


def _attn(q, k, vt, B, S, tq, tk):
    return pl.pallas_call(
        functools.partial(_attn_kernel, tk=tk, n_kv=S // tk),
        grid=(B, MLA_HEADS, S // tq),
        in_specs=[pl.BlockSpec((None, None, tq, MLA_HQ), lambda b, h, i: (b, h, i, 0)),
                  pl.BlockSpec((None, None, S, MLA_HQ), lambda b, h, i: (b, h, 0, 0)),
                  pl.BlockSpec((None, MLA_V, S), lambda b, h, i: (b, h, 0))],
        out_specs=pl.BlockSpec((None, tq, MLA_V), lambda b, h, i: (b, i, h)),
        out_shape=jax.ShapeDtypeStruct((B, S, MLA_HEADS * MLA_V), BF16),
        scratch_shapes=[pltpu.VMEM((MLA_V, tq), F32)],
        compiler_params=_params(("parallel", "parallel", "arbitrary")),
        name="mla_attn",
    )(q, k, vt)


def _post_mix_kernel(x_ref, oa_ref, ob_ref, gate_ref, wga_ref, wmo_ref, wo_ref, gf_ref, wrt_ref,
                     x1_ref, h2_ref, aff_ref):
    ya = _dot(oa_ref[...], wga_ref[...])
    yb = _dot(ob_ref[...], wmo_ref[...])
    ga = gate_ref[:, 0:D_MODEL].astype(F32)
    gb = gate_ref[:, D_MODEL:2 * D_MODEL].astype(F32)
    merged = (ga * ya + gb * yb).astype(BF16)
    x1 = x_ref[...] + _dot(merged, wo_ref[...])
    x1_ref[...] = x1
    h2 = _rms(x1, gf_ref[...]).astype(BF16)
    h2_ref[...] = h2
    logits = _dot_nt(wrt_ref[...], h2)
    e = jnp.exp(logits - jnp.max(logits, axis=0, keepdims=True))
    aff_ref[...] = e / jnp.sum(e, axis=0, keepdims=True)


def _post_mix(x2, oa, ob, gates, w_ga, w_mo, w_o, g_ffn, w_rt, B, S, tm):
    nsb = S // tm
    row = lambda n: pl.BlockSpec((tm, n), lambda i: (i, 0))
    return pl.pallas_call(
        _post_mix_kernel,
        grid=(B * nsb,),
        in_specs=[row(D_MODEL), row(D_MODEL), row(D_MODEL), row(2 * D_MODEL),
                  _const_spec(w_ga.shape), _const_spec(w_mo.shape), _const_spec(w_o.shape),
                  _const_spec(g_ffn.shape), _const_spec(w_rt.shape)],
        out_specs=[row(D_MODEL), row(D_MODEL),
                   pl.BlockSpec((None, N_EXPERTS, tm), lambda i: (i // nsb, 0, i % nsb))],
        out_shape=[jax.ShapeDtypeStruct((B * S, D_MODEL), F32),
                   jax.ShapeDtypeStruct((B * S, D_MODEL), BF16),
                   jax.ShapeDtypeStruct((B, N_EXPERTS, S), F32)],
        compiler_params=_params(("parallel",)),
        name="post_mix",
    )(x2, oa, ob, gates, w_ga, w_mo, w_o, g_ffn, w_rt)


def _select_kernel(aff_ref, pos_ref, pre_ref, *, cap, pblk):
    aff = aff_ref[...]
    S = aff.shape[1]
    bits = pltpu.bitcast(aff, jnp.int32)

    def search(i, thr):
        cand = thr | jnp.left_shift(jnp.int32(1), 30 - i)
        cnt = jnp.sum(jnp.where(bits >= cand, 1.0, 0.0), axis=1, keepdims=True)
        return jnp.where(cnt >= cap, cand, thr)

    thr = lax.fori_loop(0, 31, search, jnp.zeros((aff.shape[0], 1), jnp.int32))
    gt = bits > thr
    eq = bits == thr
    need = cap - jnp.sum(jnp.where(gt, 1.0, 0.0), axis=1, keepdims=True)

    r = lax.broadcasted_iota(jnp.int32, (pblk, pblk), 0)
    c = lax.broadcasted_iota(jnp.int32, (pblk, pblk), 1)
    before = (r < c).astype(BF16)

    def prefix_blocks(mask_of_block):
        carry = jnp.zeros((aff.shape[0], 1), F32)
        out = []
        for j in range(S // pblk):
            mk = mask_of_block(j)
            out.append(_dot(mk.astype(BF16), before) + carry)
            carry = carry + jnp.sum(mk, axis=1, keepdims=True)
        return out

    blk = lambda a, j: a[:, j * pblk:(j + 1) * pblk]
    tie_pre = prefix_blocks(lambda j: jnp.where(blk(eq, j), 1.0, 0.0))
    sel = [jnp.where(blk(gt, j) | (blk(eq, j) & (tie_pre[j] < need)), 1.0, 0.0)
           for j in range(S // pblk)]
    sel_pre = prefix_blocks(lambda j: sel[j])
    for j in range(S // pblk):
        pre_i = sel_pre[j].astype(jnp.int32)
        pre_ref[:, j * pblk:(j + 1) * pblk] = pre_i
        pos_ref[:, j * pblk:(j + 1) * pblk] = jnp.where(sel[j] > 0.5, pre_i, -1)


def _select(aff_t, cap, pblk):
    B, E, S = aff_t.shape
    spec = pl.BlockSpec((None, E, S), lambda b: (b, 0, 0))
    return pl.pallas_call(
        functools.partial(_select_kernel, cap=cap, pblk=pblk),
        grid=(B,),
        in_specs=[spec],
        out_specs=[spec, spec],
        out_shape=[jax.ShapeDtypeStruct((B, E, S), jnp.int32)] * 2,
        compiler_params=_params(("parallel",)),
        name="ec_select",
    )(aff_t)


def _ec_ffn_kernel(bs_ref, pos_ref, val_ref, h2_ref, wg_ref, wu_ref, wd_ref, x1_hbm,
                   out_ref, xe_ref, ye_ref, sem, *, n_parts, nbp, tbk, grp):
    b, part, e = pl.program_id(0), pl.program_id(1), pl.program_id(2)
    tp = nbp * tbk

    @pl.when(e == 0)
    def _():
        row0 = pl.multiple_of((b * n_parts + part) * tp, tp)
        cp = pltpu.make_async_copy(x1_hbm.at[pl.ds(row0, tp), :], out_ref, sem)
        cp.start()
        cp.wait()

    base = (b * N_EXPERTS + e) * (n_parts * nbp + 1) + part * nbp
    s_lo = bs_ref[base]
    s_hi = bs_ref[base + nbp]
    slot_iota = lax.broadcasted_iota(jnp.int32, (grp, tbk), 0)

    def group(g, _):
        gs = s_lo + g * grp
        ge = jnp.minimum(gs + grp, s_hi)
        first = lax.while_loop(lambda i: bs_ref[base + i + 1] <= gs, lambda i: i + 1, jnp.int32(0))
        in_range = lambda i: jnp.logical_and(i < nbp, bs_ref[base + jnp.minimum(i, nbp)] < ge)

        def hits(i):
            off = pl.multiple_of(i * tbk, tbk)
            return off, pos_ref[:, pl.ds(off, tbk)] == (slot_iota + gs)

        xe_ref[...] = jnp.zeros_like(xe_ref)

        def gather(i):
            off, hit = hits(i)
            onehot = jnp.where(hit, 1.0, 0.0).astype(BF16)
            xe_ref[...] += _dot(onehot, h2_ref[pl.ds(off, tbk), :])
            return i + 1

        lax.while_loop(in_range, gather, first)
        xe = xe_ref[...].astype(BF16)
        a = _dot(xe, wg_ref[...])
        u = _dot(xe, wu_ref[...])
        mid = (a * jax.nn.sigmoid(a) * u).astype(BF16)
        ye_ref[...] = _dot(mid, wd_ref[...]).astype(BF16)

        def scatter(i):
            off, hit = hits(i)
            w = jnp.where(hit, val_ref[:, pl.ds(off, tbk)], 0.0).astype(BF16)
            out_ref[pl.ds(off, tbk), :] += _dot_tn(w, ye_ref[...])
            return i + 1

        lax.while_loop(in_range, scatter, first)
        return 0

    lax.fori_loop(0, (s_hi - s_lo + grp - 1) // grp, group, 0)


def _ec_ffn(bs, pos4, val4, h2, wg, wu, wd, x1, B, S, n_parts, tbk, grp):
    tp = S // n_parts
    nbp = tp // tbk
    row_spec = pl.BlockSpec((None, None, 1, tp), lambda b, p, e, bs_: (b, e, 0, p))
    w_spec = lambda shp: pl.BlockSpec((None,) + shp, lambda b, p, e, bs_: (e, 0, 0))
    part_spec = pl.BlockSpec((tp, D_MODEL), lambda b, p, e, bs_: (b * n_parts + p, 0))
    grid_spec = pltpu.PrefetchScalarGridSpec(
        num_scalar_prefetch=1,
        grid=(B, n_parts, N_EXPERTS),
        in_specs=[row_spec, row_spec, part_spec,
                  w_spec((D_MODEL, EXPERT_FF)), w_spec((D_MODEL, EXPERT_FF)),
                  w_spec((EXPERT_FF, D_MODEL)), pl.BlockSpec(memory_space=pl.ANY)],
        out_specs=part_spec,
        scratch_shapes=[pltpu.VMEM((grp, D_MODEL), F32), pltpu.VMEM((grp, D_MODEL), BF16),
                        pltpu.SemaphoreType.DMA(())],
    )
    return pl.pallas_call(
        functools.partial(_ec_ffn_kernel, n_parts=n_parts, nbp=nbp, tbk=tbk, grp=grp),
        grid_spec=grid_spec,
        out_shape=jax.ShapeDtypeStruct((B * S, D_MODEL), F32),
        compiler_params=_params(("parallel", "parallel", "arbitrary")),
        name="ec_ffn",
    )(bs, pos4, val4, h2, wg, wu, wd, x1)


def _tiles(S):
    return dict(tm=min(256, S), tb=min(256, S), tq=min(512, S), tk=min(512, S),
                tmix=min(512, S), pblk=min(256, S), tbk=min(256, S),
                n_parts=max(1, min(4, S // 256)), grp=128)


def kernel(x, positions, g_mix, w_in, gla_wa2_f, gla_ba_f, gla_wa2_b, gla_ba_b, gla_onorm,
           w_gla_out, mla_q_norm, w_uq, mla_kv_norm, w_ukv, q_head_norm, k_head_norm, w_mla_out,
           w_o, g_ffn, w_router, w_gate_e, w_up_e, w_down_e):
    B, S, D = x.shape
    assert D == D_MODEL and S % GLA_CHUNK == 0
    t = _tiles(S)
    T = B * S
    x2 = x.reshape(T, D)
    row = lambda v: v.reshape(1, -1).astype(F32)

    nqk, nv = GLA_HEADS * GLA_DK, GLA_HEADS * GLA_DV
    o_dec = 2 * nqk + 2 * nv
    o_mla = o_dec + 2 * GLA_GATE_RANK
    o_gate = o_mla + MLA_Q_RANK + MLA_KV_RANK + MLA_ROPE
    w_main = w_in[:, :o_dec].astype(BF16)
    w_dec = jnp.pad(w_in[:, o_dec:o_mla], ((0, 0), (0, LANES - 2 * GLA_GATE_RANK))).astype(BF16)
    w_mla = jnp.pad(w_in[:, o_mla:o_gate], ((0, 0), (0, MLA_ROPE))).astype(BF16)
    w_gates = w_in[:, o_gate:].astype(BF16)
    wa2 = jnp.zeros((LANES, 2 * nqk), F32)
    wa2 = wa2.at[0:GLA_GATE_RANK, 0:nqk].set(gla_wa2_f)
    wa2 = wa2.at[GLA_GATE_RANK:2 * GLA_GATE_RANK, nqk:].set(gla_wa2_b).astype(BF16)
    ba = jnp.concatenate([gla_ba_f, gla_ba_b]).reshape(1, -1)

    gq, gk, gv, gr, la, gates = _gla_in(x2, row(g_mix), w_main, w_dec, wa2, ba, w_gates, t["tm"])

    w_uq3 = w_uq.reshape(MLA_Q_RANK, MLA_HEADS, MLA_HQ)
    w_uq_r = jnp.concatenate([w_uq3[:, :, :MLA_NOPE].reshape(MLA_Q_RANK, -1),
                              w_uq3[:, :, MLA_NOPE:].reshape(MLA_Q_RANK, -1)], axis=1).astype(BF16)
    w_ukv3 = w_ukv.reshape(MLA_KV_RANK, MLA_HEADS, MLA_NOPE + MLA_V)
    w_uk = w_ukv3[:, :, :MLA_NOPE].reshape(MLA_KV_RANK, -1).astype(BF16)
    w_vt = w_ukv3[:, :, MLA_NOPE:].reshape(MLA_KV_RANK, -1).T.astype(BF16)
    half = MLA_ROPE // 2
    freqs = ROPE_THETA ** (-jnp.arange(half, dtype=F32) / half)
    freq128 = jnp.tile(freqs, LANES // half).reshape(1, LANES)
    sign128 = jnp.tile(jnp.concatenate([-jnp.ones(half, F32), jnp.ones(half, F32)]), 2).reshape(1, LANES)
    qhr = jnp.tile(q_head_norm[MLA_NOPE:], 2).reshape(1, LANES)
    khr = jnp.pad(k_head_norm[MLA_NOPE:], (0, MLA_ROPE)).reshape(1, LANES)
    q_scale = math.log2(math.e) / math.sqrt(MLA_HQ)
    q_b, k_b, v_t = _mla_in(x2, row(g_mix), positions.reshape(T, 1).astype(F32), freq128, sign128,
                            w_mla, row(mla_q_norm), w_uq_r, row(mla_kv_norm), w_uk, w_vt,
                            row(q_head_norm[:MLA_NOPE]), qhr, row(k_head_norm[:MLA_NOPE]), khr,
                            B, S, t["tm"], q_scale)

    o_a = _gla(gq, gk, gv, la, gr, row(gla_onorm), B, S, t["tb"])
    o_b = _attn(q_b, k_b, v_t, B, S, t["tq"], t["tk"]).reshape(T, MLA_HEADS * MLA_V)

    x1, h2, aff_t = _post_mix(x2, o_a, o_b, gates, w_gla_out.astype(BF16), w_mla_out.astype(BF16),
                              w_o.astype(BF16), row(g_ffn), w_router.T.astype(BF16), B, S, t["tmix"])

    cap = EC_CAPACITY * S // N_EXPERTS
    pos, pre = _select(aff_t, cap, t["pblk"])
    tbk = t["tbk"]
    bs = jnp.concatenate([pre[:, :, ::tbk], jnp.full((B, N_EXPERTS, 1), cap, jnp.int32)], axis=2)
    out = _ec_ffn(bs.reshape(-1), pos.reshape(B, N_EXPERTS, 1, S), aff_t.reshape(B, N_EXPERTS, 1, S),
                  h2, w_gate_e.astype(BF16), w_up_e.astype(BF16), w_down_e.astype(BF16), x1,
                  B, S, t["n_parts"], tbk, t["grp"])
    return out.reshape(B, S, D)
```

```python
import functools
import math

import jax
import jax.numpy as jnp
from jax import lax
from jax.experimental import pallas as pl
from jax.experimental.pallas import tpu as pltpu

D_MODEL = 1024
GLA_HEADS = 4
GLA_DK = 128
GLA_DV = 256
GLA_GATE_RANK = 16
GLA_TAU = 16.0
GLA_CHUNK = 64
MLA_HEADS = 8
MLA_NOPE = 128
MLA_ROPE = 64
MLA_V = 128
MLA_Q_RANK = 384
MLA_KV_RANK = 256
MLA_HQ = MLA_NOPE + MLA_ROPE
MLA_VX = MLA_V + 16
ROPE_THETA = 10000.0
N_EXPERTS = 16
EC_CAPACITY = 2
EXPERT_FF = 1024
EPS = 1e-6

LANES = 128
VMEM_LIMIT = 52 * 1024 * 1024

BF16 = jnp.bfloat16
F32 = jnp.float32


def _dot(a, b):
    return jnp.dot(a, b, preferred_element_type=F32)


def _dot_nt(a, b):
    return lax.dot_general(a, b, (((1,), (1,)), ((), ())), preferred_element_type=F32)


def _dot_tn(a, b):
    return lax.dot_general(a, b, (((0,), (0,)), ((), ())), preferred_element_type=F32)


def _rms(x, gain):
    ms = jnp.mean(x * x, axis=-1, keepdims=True)
    return x * lax.rsqrt(ms + EPS) * gain


def _const_spec(shape):
    nd = len(shape)
    return pl.BlockSpec(shape, lambda *_: (0,) * nd)


def _params(sem):
    return pltpu.CompilerParams(dimension_semantics=sem, vmem_limit_bytes=VMEM_LIMIT)


def _gla_in_kernel(x_ref, g_ref, wmain_ref, wdec_ref, wa2_ref, ba_ref, wgate_ref,
                   q_ref, k_ref, v_ref, r_ref, la_ref, gate_ref):
    h = _rms(x_ref[...], g_ref[...]).astype(BF16)
    nqk = GLA_HEADS * GLA_DK
    nv = GLA_HEADS * GLA_DV
    q_ref[...] = (_dot(h, wmain_ref[:, 0:nqk]) * (GLA_DK ** -0.5)).astype(BF16)
    k_ref[...] = _dot(h, wmain_ref[:, nqk:2 * nqk]).astype(BF16)
    v_ref[...] = _dot(h, wmain_ref[:, 2 * nqk:2 * nqk + nv]).astype(BF16)
    r_ref[...] = _dot(h, wmain_ref[:, 2 * nqk + nv:2 * nqk + 2 * nv]).astype(BF16)
    dec = _dot(h, wdec_ref[...]).astype(BF16)
    z = _dot(dec, wa2_ref[...]) + ba_ref[...]
    log_sig = jnp.minimum(z, 0.0) - jnp.log1p(jnp.exp(-jnp.abs(z)))
    la_ref[...] = log_sig * (1.0 / GLA_TAU)
    gate_ref[...] = jax.nn.sigmoid(_dot(h, wgate_ref[...])).astype(BF16)


def _gla_in(x2, g_mix, w_main, w_dec, wa2, ba, w_gates, tm):
    T = x2.shape[0]
    nqk = GLA_HEADS * GLA_DK
    nv = GLA_HEADS * GLA_DV
    row = lambda n: pl.BlockSpec((tm, n), lambda i: (i, 0))
    return pl.pallas_call(
        _gla_in_kernel,
        grid=(T // tm,),
        in_specs=[row(D_MODEL), _const_spec(g_mix.shape), _const_spec(w_main.shape),
                  _const_spec(w_dec.shape), _const_spec(wa2.shape), _const_spec(ba.shape),
                  _const_spec(w_gates.shape)],
        out_specs=[row(nqk), row(nqk), row(nv), row(nv), row(2 * nqk), row(2 * D_MODEL)],
        out_shape=[jax.ShapeDtypeStruct((T, nqk), BF16), jax.ShapeDtypeStruct((T, nqk), BF16),
                   jax.ShapeDtypeStruct((T, nv), BF16), jax.ShapeDtypeStruct((T, nv), BF16),
                   jax.ShapeDtypeStruct((T, 2 * nqk), F32),
                   jax.ShapeDtypeStruct((T, 2 * D_MODEL), BF16)],
        compiler_params=_params(("parallel",)),
        name="gla_in",
    )(x2, g_mix, w_main, w_dec, wa2, ba, w_gates)


def _rope128(x, cos, sin_signed):
    lane = lax.broadcasted_iota(jnp.int32, x.shape, 1)
    first_half = (lane % MLA_ROPE) < (MLA_ROPE // 2)
    partner = jnp.where(first_half,
                        pltpu.roll(x, LANES - MLA_ROPE // 2, 1),
                        pltpu.roll(x, MLA_ROPE // 2, 1))
    return x * cos + partner * sin_signed


def _mla_in_kernel(x_ref, g_ref, pos_ref, freq_ref, sign_ref, wmla_ref, qn_ref, wuq_ref,
                   kvn_ref, wuk_ref, wvt_ref, qhn_ref, qhr_ref, khn_ref, khr_ref,
                   q_ref, k_ref, vt_ref, *, q_scale):
    h = _rms(x_ref[...], g_ref[...]).astype(BF16)
    lat = _dot(h, wmla_ref[...])
    cq = lat[:, 0:MLA_Q_RANK]
    ckv = lat[:, MLA_Q_RANK:MLA_Q_RANK + MLA_KV_RANK]
    kr2 = lat[:, MLA_Q_RANK + MLA_KV_RANK:MLA_Q_RANK + MLA_KV_RANK + LANES]

    ang = pos_ref[...] * freq_ref[...]
    cos = jnp.cos(ang)
    sin_signed = jnp.sin(ang) * sign_ref[...]
    lane = lax.broadcasted_iota(jnp.int32, ang.shape, 1)
    low = lane < MLA_ROPE

    nn = MLA_HEADS * MLA_NOPE
    qf = _dot(_rms(cq, qn_ref[...]).astype(BF16), wuq_ref[...])
    ckvn = _rms(ckv, kvn_ref[...]).astype(BF16)
    kn = _dot(ckvn, wuk_ref[...])
    vt = _dot_nt(wvt_ref[...], ckvn).astype(BF16)
    pad_row = lax.broadcasted_iota(jnp.int32, (MLA_VX - MLA_V, vt.shape[1]), 0)
    ones_then_zeros = jnp.where(pad_row == 0, 1.0, 0.0).astype(BF16)
    for hd in range(MLA_HEADS):
        vt_ref[hd, 0:MLA_V, :] = vt[hd * MLA_V:(hd + 1) * MLA_V, :]
        vt_ref[hd, MLA_V:MLA_VX, :] = ones_then_zeros

    ss_kr = jnp.sum(kr2 * kr2, axis=-1, keepdims=True)
    kr_rot = _rope128(kr2 * khr_ref[...], cos, sin_signed)[:, 0:MLA_ROPE]

    for pair in range(MLA_HEADS // 2):
        xr = qf[:, nn + pair * LANES: nn + (pair + 1) * LANES]
        sq = xr * xr
        ss_lo = jnp.sum(jnp.where(low, sq, 0.0), axis=-1, keepdims=True)
        ss_hi = jnp.sum(sq, axis=-1, keepdims=True) - ss_lo
        rstd = []
        for j, ss_r in enumerate((ss_lo, ss_hi)):
            hd = 2 * pair + j
            xn = qf[:, hd * MLA_NOPE:(hd + 1) * MLA_NOPE]
            ss = jnp.sum(xn * xn, axis=-1, keepdims=True) + ss_r
            r = lax.rsqrt(ss * (1.0 / MLA_HQ) + EPS)
            rstd.append(r)
            q_ref[hd, :, 0:MLA_NOPE] = (xn * r * (qhn_ref[...] * q_scale)).astype(BF16)
        scaled = xr * jnp.where(low, rstd[0], rstd[1]) * (qhr_ref[...] * q_scale)
        rot = _rope128(scaled, cos, sin_signed).astype(BF16)
        q_ref[2 * pair, :, MLA_NOPE:MLA_HQ] = rot[:, 0:MLA_ROPE]
        q_ref[2 * pair + 1, :, MLA_NOPE:MLA_HQ] = rot[:, MLA_ROPE:LANES]

    for hd in range(MLA_HEADS):
        xn = kn[:, hd * MLA_NOPE:(hd + 1) * MLA_NOPE]
        ss = jnp.sum(xn * xn, axis=-1, keepdims=True) + ss_kr
        r = lax.rsqrt(ss * (1.0 / MLA_HQ) + EPS)
        k_ref[hd, :, 0:MLA_NOPE] = (xn * r * khn_ref[...]).astype(BF16)
        k_ref[hd, :, MLA_NOPE:MLA_HQ] = (kr_rot * r).astype(BF16)


def _mla_in(x2, g_mix, pos, freq, sign, w_mla, qn, w_uq, kvn, w_uk, w_vt, qhn, qhr, khn, khr,
            B, S, tm, q_scale):
    nsb = S // tm
    consts = (g_mix, None, freq, sign, w_mla, qn, w_uq, kvn, w_uk, w_vt, qhn, qhr, khn, khr)
    in_specs = [pl.BlockSpec((tm, D_MODEL), lambda i: (i, 0))]
    for c in consts:
        in_specs.append(pl.BlockSpec((tm, 1), lambda i: (i, 0)) if c is None else _const_spec(c.shape))
    head_spec = pl.BlockSpec((None, MLA_HEADS, tm, MLA_HQ), lambda i: (i // nsb, 0, i % nsb, 0))
    return pl.pallas_call(
        functools.partial(_mla_in_kernel, q_scale=q_scale),
        grid=(B * nsb,),
        in_specs=in_specs,
        out_specs=[head_spec, head_spec,
                   pl.BlockSpec((None, MLA_HEADS, MLA_VX, tm), lambda i: (i // nsb, 0, 0, i % nsb))],
        out_shape=[jax.ShapeDtypeStruct((B, MLA_HEADS, S, MLA_HQ), BF16),
                   jax.ShapeDtypeStruct((B, MLA_HEADS, S, MLA_HQ), BF16),
                   jax.ShapeDtypeStruct((B, MLA_HEADS, MLA_VX, S), BF16)],
        compiler_params=_params(("parallel",)),
        name="mla_in",
    )(x2, g_mix, pos, freq, sign, w_mla, qn, w_uq, kvn, w_uk, w_vt, qhn, qhr, khn, khr)


def _gla_block(q_ref, k_ref, v_ref, la_ref, st_ref, emit, *, reverse, la_off, n_chunks):
    C = GLA_CHUNK
    row = lax.broadcasted_iota(jnp.int32, (C, C), 0)
    col = lax.broadcasted_iota(jnp.int32, (C, C), 1)
    if reverse:
        cum_mat = (col >= row).astype(BF16)
        att_mask = col > row
        last = 0
    else:
        cum_mat = (col <= row).astype(BF16)
        att_mask = col <= row
        last = C - 1
    order = range(n_chunks - 1, -1, -1) if reverse else range(n_chunks)
    for c in order:
        rows = slice(c * C, (c + 1) * C)
        la = la_ref[rows, la_off:la_off + GLA_HEADS * GLA_DK]
        la_hi = la.astype(BF16)
        la_lo = (la - la_hi.astype(F32)).astype(BF16)
        b_all = _dot(cum_mat, la_hi) + _dot(cum_mat, la_lo)
        for hd in range(GLA_HEADS):
            kcols = slice(hd * GLA_DK, (hd + 1) * GLA_DK)
            vcols = slice(hd * GLA_DV, (hd + 1) * GLA_DV)
            b = b_all[:, kcols]
            b_last = b[last:last + 1, :]
            q = q_ref[rows, kcols].astype(F32)
            k = k_ref[rows, kcols].astype(F32)
            v = v_ref[rows, vcols]
            q_t = (q * jnp.exp(b)).astype(BF16)
            k_t = (k * jnp.exp(-b)).astype(BF16)
            k_dec = (k * jnp.exp(b_last - b)).astype(BF16)
            att = jnp.where(att_mask, _dot_nt(q_t, k_t), 0.0).astype(BF16)
            st = st_ref[hd]
            o = _dot(att, v) + _dot_nt(q_t, st.astype(BF16))
            st_ref[hd] = st * jnp.exp(b_last) + _dot_tn(v, k_dec)
            emit(rows, vcols, o)


def _gla_fwd_kernel(q_ref, k_ref, v_ref, la_ref, o_ref, st_ref, *, n_chunks):
    @pl.when(pl.program_id(1) == 0)
    def _():
        st_ref[...] = jnp.zeros_like(st_ref)

    def emit(rows, vcols, o):
        o_ref[rows, vcols] = o

    _gla_block(q_ref, k_ref, v_ref, la_ref, st_ref, emit, reverse=False, la_off=0, n_chunks=n_chunks)


def _gla_bwd_kernel(q_ref, k_ref, v_ref, la_ref, of_ref, r_ref, gn_ref, o_ref, st_ref, *, n_chunks):
    @pl.when(pl.program_id(1) == 0)
    def _():
        st_ref[...] = jnp.zeros_like(st_ref)

    def emit(rows, vcols, o):
        tot = of_ref[rows, vcols] + o
        gate = r_ref[rows, vcols].astype(F32)
        o_ref[rows, vcols] = (_rms(tot, gn_ref[...]) * (gate * jax.nn.sigmoid(gate))).astype(BF16)

    _gla_block(q_ref, k_ref, v_ref, la_ref, st_ref, emit, reverse=True,
               la_off=GLA_HEADS * GLA_DK, n_chunks=n_chunks)


def _gla(gq, gk, gv, la, gr, gnorm, B, S, tb):
    nb = S // tb
    nqk = GLA_HEADS * GLA_DK
    nv = GLA_HEADS * GLA_DV
    n_chunks = tb // GLA_CHUNK
    state = pltpu.VMEM((GLA_HEADS, GLA_DV, GLA_DK), F32)
    fwd = lambda n: pl.BlockSpec((tb, n), lambda b, i: (b * nb + i, 0))
    rev = lambda n: pl.BlockSpec((tb, n), lambda b, i: (b * nb + nb - 1 - i, 0))
    o_f = pl.pallas_call(
        functools.partial(_gla_fwd_kernel, n_chunks=n_chunks),
        grid=(B, nb),
        in_specs=[fwd(nqk), fwd(nqk), fwd(nv), fwd(2 * nqk)],
        out_specs=fwd(nv),
        out_shape=jax.ShapeDtypeStruct((B * S, nv), F32),
        scratch_shapes=[state],
        compiler_params=_params(("parallel", "arbitrary")),
        name="gla_fwd",
    )(gq, gk, gv, la)
    return pl.pallas_call(
        functools.partial(_gla_bwd_kernel, n_chunks=n_chunks),
        grid=(B, nb),
        in_specs=[rev(nqk), rev(nqk), rev(nv), rev(2 * nqk), rev(nv), rev(nv),
                  pl.BlockSpec(gnorm.shape, lambda b, i: (0, 0))],
        out_specs=rev(nv),
        out_shape=jax.ShapeDtypeStruct((B * S, nv), BF16),
        scratch_shapes=[state],
        compiler_params=_params(("parallel", "arbitrary")),
        name="gla_bwd",
    )(gq, gk, gv, la, o_f, gr, gnorm)


def _attn_kernel(q_ref, k_ref, vt_ref, o_ref, acc_ref, s_ref, p_ref, mx_ref, al_ref, *, tk, n_kv):
    q = q_ref[...]
    tq = q.shape[0]
    n_pairs = n_kv // 2

    def scores(pair, sw):
        for t in range(2):
            off = pl.multiple_of((2 * pair + t) * tk, tk)
            s = _dot_nt(k_ref[pl.ds(off, tk), :], q)
            s_ref[2 * sw + t] = s
            mx_ref[2 * sw + t] = jnp.max(s, axis=0, keepdims=True)

    def softmax(sw, m):
        for t in range(2):
            slot = 2 * sw + t
            m_new = jnp.maximum(m, mx_ref[slot])
            al_ref[slot] = jnp.exp2(m - m_new)
            p_ref[slot] = jnp.exp2(s_ref[slot] - m_new).astype(BF16)
            m = m_new
        return m

    def accumulate(pair, sw):
        for t in range(2):
            slot = 2 * sw + t
            off = pl.multiple_of((2 * pair + t) * tk, tk)
            acc_ref[...] = al_ref[slot] * acc_ref[...] + _dot(vt_ref[:, pl.ds(off, tk)], p_ref[slot])

    acc_ref[...] = jnp.zeros_like(acc_ref)
    scores(0, 0)
    scores(1, 1)
    m = softmax(0, jnp.full((1, tq), -1e30, F32))

    def body(it, m):
        pair = 1 + 2 * it
        scores(pair + 1, 0)
        m = softmax(1, m)
        accumulate(pair - 1, 0)
        scores(pair + 2, 1)
        m = softmax(0, m)
        accumulate(pair, 1)
        return m

    m = lax.fori_loop(0, (n_pairs - 2) // 2, body, m)
    softmax(1, m)
    accumulate(n_pairs - 2, 0)
    accumulate(n_pairs - 1, 1)
    inv_l = 1.0 / acc_ref[MLA_V:MLA_V + 1, :]
    o_ref[...] = (acc_ref[0:MLA_V, :] * inv_l).T.astype(BF16)


def _attn(q, k, vt, B, S, tq, tk):
    assert (S // tk) % 4 == 0, "kv tiles are consumed in pairs, two pairs per loop pass"
    return pl.pallas_call(
        functools.partial(_attn_kernel, tk=tk, n_kv=S // tk),
        grid=(B, MLA_HEADS, S // tq),
        in_specs=[pl.BlockSpec((None, None, tq, MLA_HQ), lambda b, h, i: (b, h, i, 0)),
                  pl.BlockSpec((None, None, S, MLA_HQ), lambda b, h, i: (b, h, 0, 0)),
                  pl.BlockSpec((None, None, MLA_VX, S), lambda b, h, i: (b, h, 0, 0))],
        out_specs=pl.BlockSpec((None, tq, MLA_V), lambda b, h, i: (b, i, h)),
        out_shape=jax.ShapeDtypeStruct((B, S, MLA_HEADS * MLA_V), BF16),
        scratch_shapes=[pltpu.VMEM((MLA_VX, tq), F32), pltpu.VMEM((4, tk, tq), F32),
                        pltpu.VMEM((4, tk, tq), BF16), pltpu.VMEM((4, 1, tq), F32),
                        pltpu.VMEM((4, 1, tq), F32)],
        compiler_params=_params(("parallel", "parallel", "arbitrary")),
        name="mla_attn",
    )(q, k, vt)


def _post_mix_kernel(x_ref, oa_ref, ob_ref, gate_ref, wga_ref, wmo_ref, wo_ref, gf_ref, wrt_ref,
                     x1_ref, h2_ref, aff_ref):
    ya = _dot(oa_ref[...], wga_ref[...])
    yb = _dot(ob_ref[...], wmo_ref[...])
    ga = gate_ref[:, 0:D_MODEL].astype(F32)
    gb = gate_ref[:, D_MODEL:2 * D_MODEL].astype(F32)
    merged = (ga * ya + gb * yb).astype(BF16)
    x1 = x_ref[...] + _dot(merged, wo_ref[...])
    x1_ref[...] = x1
    h2 = _rms(x1, gf_ref[...]).astype(BF16)
    h2_ref[...] = h2
    logits = _dot_nt(wrt_ref[...], h2)
    e = jnp.exp(logits - jnp.max(logits, axis=0, keepdims=True))
    aff_ref[...] = e / jnp.sum(e, axis=0, keepdims=True)


def _post_mix(x2, oa, ob, gates, w_ga, w_mo, w_o, g_ffn, w_rt, B, S, tm):
    nsb = S // tm
    row = lambda n: pl.BlockSpec((tm, n), lambda i: (i, 0))
    return pl.pallas_call(
        _post_mix_kernel,
        grid=(B * nsb,),
        in_specs=[row(D_MODEL), row(D_MODEL), row(D_MODEL), row(2 * D_MODEL),
                  _const_spec(w_ga.shape), _const_spec(w_mo.shape), _const_spec(w_o.shape),
                  _const_spec(g_ffn.shape), _const_spec(w_rt.shape)],
        out_specs=[row(D_MODEL), row(D_MODEL),
                   pl.BlockSpec((None, N_EXPERTS, tm), lambda i: (i // nsb, 0, i % nsb))],
        out_shape=[jax.ShapeDtypeStruct((B * S, D_MODEL), F32),
                   jax.ShapeDtypeStruct((B * S, D_MODEL), BF16),
                   jax.ShapeDtypeStruct((B, N_EXPERTS, S), F32)],
        compiler_params=_params(("parallel",)),
        name="post_mix",
    )(x2, oa, ob, gates, w_ga, w_mo, w_o, g_ffn, w_rt)


def _select_kernel(aff_ref, pos_ref, pre_ref, *, cap, pblk):
    aff = aff_ref[...]
    S = aff.shape[1]
    bits = pltpu.bitcast(aff, jnp.int32)

    def search(i, thr):
        cand = thr | jnp.left_shift(jnp.int32(1), 30 - i)
        cnt = jnp.sum(jnp.where(bits >= cand, 1.0, 0.0), axis=1, keepdims=True)
        return jnp.where(cnt >= cap, cand, thr)

    thr = lax.fori_loop(0, 31, search, jnp.zeros((aff.shape[0], 1), jnp.int32))
    gt = bits > thr
    eq = bits == thr
    need = cap - jnp.sum(jnp.where(gt, 1.0, 0.0), axis=1, keepdims=True)

    r = lax.broadcasted_iota(jnp.int32, (pblk, pblk), 0)
    c = lax.broadcasted_iota(jnp.int32, (pblk, pblk), 1)
    before = (r < c).astype(BF16)

    def prefix_blocks(mask_of_block):
        carry = jnp.zeros((aff.shape[0], 1), F32)
        out = []
        for j in range(S // pblk):
            mk = mask_of_block(j)
            out.append(_dot(mk.astype(BF16), before) + carry)
            carry = carry + jnp.sum(mk, axis=1, keepdims=True)
        return out

    blk = lambda a, j: a[:, j * pblk:(j + 1) * pblk]
    tie_pre = prefix_blocks(lambda j: jnp.where(blk(eq, j), 1.0, 0.0))
    sel = [jnp.where(blk(gt, j) | (blk(eq, j) & (tie_pre[j] < need)), 1.0, 0.0)
           for j in range(S // pblk)]
    sel_pre = prefix_blocks(lambda j: sel[j])
    for j in range(S // pblk):
        pre_i = sel_pre[j].astype(jnp.int32)
        pre_ref[:, j * pblk:(j + 1) * pblk] = pre_i
        pos_ref[:, j * pblk:(j + 1) * pblk] = jnp.where(sel[j] > 0.5, pre_i, -1)


def _select(aff_t, cap, pblk):
    B, E, S = aff_t.shape
    spec = pl.BlockSpec((None, E, S), lambda b: (b, 0, 0))
    return pl.pallas_call(
        functools.partial(_select_kernel, cap=cap, pblk=pblk),
        grid=(B,),
        in_specs=[spec],
        out_specs=[spec, spec],
        out_shape=[jax.ShapeDtypeStruct((B, E, S), jnp.int32)] * 2,
        compiler_params=_params(("parallel",)),
        name="ec_select",
    )(aff_t)


def _ec_ffn_kernel(bs_ref, pos_ref, val_ref, h2_ref, wg_ref, wu_ref, wd_ref, x1_hbm,
                   out_ref, xe_ref, ye_ref, sem, *, n_parts, nbp, tbk, grp):
    b, part, e = pl.program_id(0), pl.program_id(1), pl.program_id(2)
    tp = nbp * tbk

    @pl.when(e == 0)
    def _():
        row0 = pl.multiple_of((b * n_parts + part) * tp, tp)
        cp = pltpu.make_async_copy(x1_hbm.at[pl.ds(row0, tp), :], out_ref, sem)
        cp.start()
        cp.wait()

    base = (b * N_EXPERTS + e) * (n_parts * nbp + 1) + part * nbp
    s_lo = bs_ref[base]
    s_hi = bs_ref[base + nbp]
    slot_iota = lax.broadcasted_iota(jnp.int32, (grp, tbk), 0)

    def group(g, _):
        gs = s_lo + g * grp
        ge = jnp.minimum(gs + grp, s_hi)
        first = lax.while_loop(lambda i: bs_ref[base + i + 1] <= gs, lambda i: i + 1, jnp.int32(0))
        in_range = lambda i: jnp.logical_and(i < nbp, bs_ref[base + jnp.minimum(i, nbp)] < ge)

        def hits(i):
            off = pl.multiple_of(i * tbk, tbk)
            return off, pos_ref[:, pl.ds(off, tbk)] == (slot_iota + gs)

        xe_ref[...] = jnp.zeros_like(xe_ref)

        def gather(i):
            off, hit = hits(i)
            onehot = jnp.where(hit, 1.0, 0.0).astype(BF16)
            xe_ref[...] += _dot(onehot, h2_ref[pl.ds(off, tbk), :])
            return i + 1

        lax.while_loop(in_range, gather, first)
        xe = xe_ref[...].astype(BF16)
        a = _dot(xe, wg_ref[...])
        u = _dot(xe, wu_ref[...])
        mid = (a * jax.nn.sigmoid(a) * u).astype(BF16)
        ye_ref[...] = _dot(mid, wd_ref[...]).astype(BF16)

        def scatter(i):
            off, hit = hits(i)
            w = jnp.where(hit, val_ref[:, pl.ds(off, tbk)], 0.0).astype(BF16)
            out_ref[pl.ds(off, tbk), :] += _dot_tn(w, ye_ref[...])
            return i + 1

        lax.while_loop(in_range, scatter, first)
        return 0

    lax.fori_loop(0, (s_hi - s_lo + grp - 1) // grp, group, 0)


def _ec_ffn(bs, pos4, val4, h2, wg, wu, wd, x1, B, S, n_parts, tbk, grp):
    tp = S // n_parts
    nbp = tp // tbk
    row_spec = pl.BlockSpec((None, None, 1, tp), lambda b, p, e, bs_: (b, e, 0, p))
    w_spec = lambda shp: pl.BlockSpec((None,) + shp, lambda b, p, e, bs_: (e, 0, 0))
    part_spec = pl.BlockSpec((tp, D_MODEL), lambda b, p, e, bs_: (b * n_parts + p, 0))
    grid_spec = pltpu.PrefetchScalarGridSpec(
        num_scalar_prefetch=1,
        grid=(B, n_parts, N_EXPERTS),
        in_specs=[row_spec, row_spec, part_spec,
                  w_spec((D_MODEL, EXPERT_FF)), w_spec((D_MODEL, EXPERT_FF)),
                  w_spec((EXPERT_FF, D_MODEL)), pl.BlockSpec(memory_space=pl.ANY)],
        out_specs=part_spec,
        scratch_shapes=[pltpu.VMEM((grp, D_MODEL), F32), pltpu.VMEM((grp, D_MODEL), BF16),
                        pltpu.SemaphoreType.DMA(())],
    )
    return pl.pallas_call(
        functools.partial(_ec_ffn_kernel, n_parts=n_parts, nbp=nbp, tbk=tbk, grp=grp),
        grid_spec=grid_spec,
        out_shape=jax.ShapeDtypeStruct((B * S, D_MODEL), F32),
        compiler_params=_params(("parallel", "parallel", "arbitrary")),
        name="ec_ffn",
    )(bs, pos4, val4, h2, wg, wu, wd, x1)


def _tiles(S):
    return dict(tm=min(256, S), tb=min(256, S), tq=min(512, S), tk=min(512, S // 4),
                tmix=min(512, S), pblk=min(256, S), tbk=min(256, S),
                n_parts=max(1, min(4, S // 256)), grp=128)


def kernel(x, positions, g_mix, w_in, gla_wa2_f, gla_ba_f, gla_wa2_b, gla_ba_b, gla_onorm,
           w_gla_out, mla_q_norm, w_uq, mla_kv_norm, w_ukv, q_head_norm, k_head_norm, w_mla_out,
           w_o, g_ffn, w_router, w_gate_e, w_up_e, w_down_e):
    B, S, D = x.shape
    assert D == D_MODEL and S % GLA_CHUNK == 0
    t = _tiles(S)
    T = B * S
    x2 = x.reshape(T, D)
    row = lambda v: v.reshape(1, -1).astype(F32)

    nqk, nv = GLA_HEADS * GLA_DK, GLA_HEADS * GLA_DV
    o_dec = 2 * nqk + 2 * nv
    o_mla = o_dec + 2 * GLA_GATE_RANK
    o_gate = o_mla + MLA_Q_RANK + MLA_KV_RANK + MLA_ROPE
    w_main = w_in[:, :o_dec].astype(BF16)
    w_dec = jnp.pad(w_in[:, o_dec:o_mla], ((0, 0), (0, LANES - 2 * GLA_GATE_RANK))).astype(BF16)
    w_mla = jnp.pad(w_in[:, o_mla:o_gate], ((0, 0), (0, MLA_ROPE))).astype(BF16)
    w_gates = w_in[:, o_gate:].astype(BF16)
    wa2 = jnp.zeros((LANES, 2 * nqk), F32)
    wa2 = wa2.at[0:GLA_GATE_RANK, 0:nqk].set(gla_wa2_f)
    wa2 = wa2.at[GLA_GATE_RANK:2 * GLA_GATE_RANK, nqk:].set(gla_wa2_b).astype(BF16)
    ba = jnp.concatenate([gla_ba_f, gla_ba_b]).reshape(1, -1)

    gq, gk, gv, gr, la, gates = _gla_in(x2, row(g_mix), w_main, w_dec, wa2, ba, w_gates, t["tm"])

    w_uq3 = w_uq.reshape(MLA_Q_RANK, MLA_HEADS, MLA_HQ)
    w_uq_r = jnp.concatenate([w_uq3[:, :, :MLA_NOPE].reshape(MLA_Q_RANK, -1),
                              w_uq3[:, :, MLA_NOPE:].reshape(MLA_Q_RANK, -1)], axis=1).astype(BF16)
    w_ukv3 = w_ukv.reshape(MLA_KV_RANK, MLA_HEADS, MLA_NOPE + MLA_V)
    w_uk = w_ukv3[:, :, :MLA_NOPE].reshape(MLA_KV_RANK, -1).astype(BF16)
    w_vt = w_ukv3[:, :, MLA_NOPE:].reshape(MLA_KV_RANK, -1).T.astype(BF16)
    half = MLA_ROPE // 2
    freqs = ROPE_THETA ** (-jnp.arange(half, dtype=F32) / half)
    freq128 = jnp.tile(freqs, LANES // half).reshape(1, LANES)
    sign128 = jnp.tile(jnp.concatenate([-jnp.ones(half, F32), jnp.ones(half, F32)]), 2).reshape(1, LANES)
    qhr = jnp.tile(q_head_norm[MLA_NOPE:], 2).reshape(1, LANES)
    khr = jnp.pad(k_head_norm[MLA_NOPE:], (0, MLA_ROPE)).reshape(1, LANES)
    q_scale = math.log2(math.e) / math.sqrt(MLA_HQ)
    q_b, k_b, v_t = _mla_in(x2, row(g_mix), positions.reshape(T, 1).astype(F32), freq128, sign128,
                            w_mla, row(mla_q_norm), w_uq_r, row(mla_kv_norm), w_uk, w_vt,
                            row(q_head_norm[:MLA_NOPE]), qhr, row(k_head_norm[:MLA_NOPE]), khr,
                            B, S, t["tm"], q_scale)

    o_a = _gla(gq, gk, gv, la, gr, row(gla_onorm), B, S, t["tb"])
    o_b = _attn(q_b, k_b, v_t, B, S, t["tq"], t["tk"]).reshape(T, MLA_HEADS * MLA_V)

    x1, h2, aff_t = _post_mix(x2, o_a, o_b, gates, w_gla_out.astype(BF16), w_mla_out.astype(BF16),
                              w_o.astype(BF16), row(g_ffn), w_router.T.astype(BF16), B, S, t["tmix"])

    cap = EC_CAPACITY * S // N_EXPERTS
    pos, pre = _select(aff_t, cap, t["pblk"])
    tbk = t["tbk"]
    bs = jnp.concatenate([pre[:, :, ::tbk], jnp.full((B, N_EXPERTS, 1), cap, jnp.int32)], axis=2)
    out = _ec_ffn(bs.reshape(-1), pos.reshape(B, N_EXPERTS, 1, S), aff_t.reshape(B, N_EXPERTS, 1, S),
                  h2, w_gate_e.astype(BF16), w_up_e.astype(BF16), w_down_e.astype(BF16), x1,
                  B, S, t["n_parts"], tbk, t["grp"])
    return out.reshape(B, S, D)
```

```python
import functools
import math

import jax
import jax.numpy as jnp
from jax import lax
from jax.experimental import pallas as pl
from jax.experimental.pallas import tpu as pltpu

D_MODEL = 1024
GLA_HEADS = 4
GLA_DK = 128
GLA_DV = 256
GLA_GATE_RANK = 16
GLA_TAU = 16.0
GLA_CHUNK = 64
MLA_HEADS = 8
MLA_NOPE = 128
MLA_ROPE = 64
MLA_V = 128
MLA_Q_RANK = 384
MLA_KV_RANK = 256
MLA_HQ = MLA_NOPE + MLA_ROPE
MLA_VX = MLA_V + 16
ROPE_THETA = 10000.0
N_EXPERTS = 16
EC_CAPACITY = 2
EXPERT_FF = 1024
EPS = 1e-6

LANES = 128
VMEM_LIMIT = 52 * 1024 * 1024

BF16 = jnp.bfloat16
F32 = jnp.float32


def _dot(a, b):
    return jnp.dot(a, b, preferred_element_type=F32)


def _dot_nt(a, b):
    return lax.dot_general(a, b, (((1,), (1,)), ((), ())), preferred_element_type=F32)


def _dot_tn(a, b):
    return lax.dot_general(a, b, (((0,), (0,)), ((), ())), preferred_element_type=F32)


def _rms(x, gain):
    ms = jnp.mean(x * x, axis=-1, keepdims=True)
    return x * lax.rsqrt(ms + EPS) * gain


def _sigmoid(x):
    return 0.5 * jnp.tanh(0.5 * x) + 0.5


def _const_spec(shape):
    nd = len(shape)
    return pl.BlockSpec(shape, lambda *_: (0,) * nd, pipeline_mode=pl.Buffered(1))


def _params(sem):
    return pltpu.CompilerParams(dimension_semantics=sem, vmem_limit_bytes=VMEM_LIMIT)


def _gla_in_kernel(x_ref, g_ref, wmain_ref, wdec_ref, wa2_ref, ba_ref, wgate_ref,
                   q_ref, k_ref, v_ref, r_ref, la_ref, gate_ref):
    h = _rms(x_ref[...], g_ref[...]).astype(BF16)
    nqk = GLA_HEADS * GLA_DK
    nv = GLA_HEADS * GLA_DV
    dec = _dot(h, wdec_ref[...]).astype(BF16)
    q_ref[...] = (_dot(h, wmain_ref[:, 0:nqk]) * (GLA_DK ** -0.5)).astype(BF16)
    k_ref[...] = _dot(h, wmain_ref[:, nqk:2 * nqk]).astype(BF16)
    z = _dot(dec, wa2_ref[...]) + ba_ref[...]
    log_sig = jnp.minimum(z, 0.0) - jnp.log1p(jnp.exp(-jnp.abs(z)))
    la_ref[...] = log_sig * (1.0 / GLA_TAU)
    v_ref[...] = _dot(h, wmain_ref[:, 2 * nqk:2 * nqk + nv]).astype(BF16)
    r_ref[...] = _dot(h, wmain_ref[:, 2 * nqk + nv:2 * nqk + 2 * nv]).astype(BF16)
    gate_ref[...] = _sigmoid(_dot(h, wgate_ref[...])).astype(BF16)


def _gla_in(x2, g_mix, w_main, w_dec, wa2, ba, w_gates, tm):
    T = x2.shape[0]
    nqk = GLA_HEADS * GLA_DK
    nv = GLA_HEADS * GLA_DV
    row = lambda n: pl.BlockSpec((tm, n), lambda i: (i, 0))
    return pl.pallas_call(
        _gla_in_kernel,
        grid=(T // tm,),
        in_specs=[row(D_MODEL), _const_spec(g_mix.shape), _const_spec(w_main.shape),
                  _const_spec(w_dec.shape), _const_spec(wa2.shape), _const_spec(ba.shape),
                  _const_spec(w_gates.shape)],
        out_specs=[row(nqk), row(nqk), row(nv), row(nv), row(2 * nqk), row(2 * D_MODEL)],
        out_shape=[jax.ShapeDtypeStruct((T, nqk), BF16), jax.ShapeDtypeStruct((T, nqk), BF16),
                   jax.ShapeDtypeStruct((T, nv), BF16), jax.ShapeDtypeStruct((T, nv), BF16),
                   jax.ShapeDtypeStruct((T, 2 * nqk), F32),
                   jax.ShapeDtypeStruct((T, 2 * D_MODEL), BF16)],
        compiler_params=_params(("parallel",)),
        name="gla_in",
    )(x2, g_mix, w_main, w_dec, wa2, ba, w_gates)


def _rope128(x, cos, sin_signed):
    lane = lax.broadcasted_iota(jnp.int32, x.shape, 1)
    first_half = (lane % MLA_ROPE) < (MLA_ROPE // 2)
    partner = jnp.where(first_half,
                        pltpu.roll(x, LANES - MLA_ROPE // 2, 1),
                        pltpu.roll(x, MLA_ROPE // 2, 1))
    return x * cos + partner * sin_signed


def _mla_in_kernel(x_ref, g_ref, pos_ref, freq_ref, sign_ref, wmla_ref, qn_ref, wuq_ref,
                   kvn_ref, wuk_ref, wvt_ref, qhn_ref, qhr_ref, khn_ref, khr_ref,
                   q_ref, k_ref, vt_ref, *, q_scale):
    h = _rms(x_ref[...], g_ref[...]).astype(BF16)
    lat = _dot(h, wmla_ref[...])
    cq = lat[:, 0:MLA_Q_RANK]
    ckv = lat[:, MLA_Q_RANK:MLA_Q_RANK + MLA_KV_RANK]
    kr2 = lat[:, MLA_Q_RANK + MLA_KV_RANK:MLA_Q_RANK + MLA_KV_RANK + LANES]

    ang = pos_ref[...] * freq_ref[...]
    cos = jnp.cos(ang)
    sin_signed = jnp.sin(ang) * sign_ref[...]
    lane = lax.broadcasted_iota(jnp.int32, ang.shape, 1)
    low = lane < MLA_ROPE

    nn = MLA_HEADS * MLA_NOPE
    qf = _dot(_rms(cq, qn_ref[...]).astype(BF16), wuq_ref[...])
    ckvn = _rms(ckv, kvn_ref[...]).astype(BF16)
    kn = _dot(ckvn, wuk_ref[...])
    vt = _dot_nt(wvt_ref[...], ckvn).astype(BF16)
    pad_row = lax.broadcasted_iota(jnp.int32, (MLA_VX - MLA_V, vt.shape[1]), 0)
    ones_then_zeros = jnp.where(pad_row == 0, 1.0, 0.0).astype(BF16)
    for hd in range(MLA_HEADS):
        vt_ref[hd, 0:MLA_V, :] = vt[hd * MLA_V:(hd + 1) * MLA_V, :]
        vt_ref[hd, MLA_V:MLA_VX, :] = ones_then_zeros

    ss_kr = jnp.sum(kr2 * kr2, axis=-1, keepdims=True)
    kr_rot = _rope128(kr2 * khr_ref[...], cos, sin_signed)[:, 0:MLA_ROPE]

    for pair in range(MLA_HEADS // 2):
        xr = qf[:, nn + pair * LANES: nn + (pair + 1) * LANES]
        sq = xr * xr
        ss_lo = jnp.sum(jnp.where(low, sq, 0.0), axis=-1, keepdims=True)
        ss_hi = jnp.sum(sq, axis=-1, keepdims=True) - ss_lo
        rstd = []
        for j, ss_r in enumerate((ss_lo, ss_hi)):
            hd = 2 * pair + j
            xn = qf[:, hd * MLA_NOPE:(hd + 1) * MLA_NOPE]
            ss = jnp.sum(xn * xn, axis=-1, keepdims=True) + ss_r
            r = lax.rsqrt(ss * (1.0 / MLA_HQ) + EPS)
            rstd.append(r)
            q_ref[hd, :, 0:MLA_NOPE] = (xn * r * (qhn_ref[...] * q_scale)).astype(BF16)
        scaled = xr * jnp.where(low, rstd[0], rstd[1]) * (qhr_ref[...] * q_scale)
        rot = _rope128(scaled, cos, sin_signed).astype(BF16)
        q_ref[2 * pair, :, MLA_NOPE:MLA_HQ] = rot[:, 0:MLA_ROPE]
        q_ref[2 * pair + 1, :, MLA_NOPE:MLA_HQ] = rot[:, MLA_ROPE:LANES]

    for hd in range(MLA_HEADS):
        xn = kn[:, hd * MLA_NOPE:(hd + 1) * MLA_NOPE]
        ss = jnp.sum(xn * xn, axis=-1, keepdims=True) + ss_kr
        r = lax.rsqrt(ss * (1.0 / MLA_HQ) + EPS)
        k_ref[hd, :, 0:MLA_NOPE] = (xn * r * khn_ref[...]).astype(BF16)
        k_ref[hd, :, MLA_NOPE:MLA_HQ] = (kr_rot * r).astype(BF16)


def _mla_in(x2, g_mix, pos, freq, sign, w_mla, qn, w_uq, kvn, w_uk, w_vt, qhn, qhr, khn, khr,
            B, S, tm, tk, q_scale):
    nsb = S // tm
    per_kv = tk // tm
    assert tk % tm == 0
    consts = (g_mix, None, freq, sign, w_mla, qn, w_uq, kvn, w_uk, w_vt, qhn, qhr, khn, khr)
    in_specs = [pl.BlockSpec((tm, D_MODEL), lambda i: (i, 0))]
    for c in consts:
        in_specs.append(pl.BlockSpec((tm, 1), lambda i: (i, 0)) if c is None else _const_spec(c.shape))
    head_spec = pl.BlockSpec((None, MLA_HEADS, tm, MLA_HQ), lambda i: (i // nsb, 0, i % nsb, 0))
    return pl.pallas_call(
        functools.partial(_mla_in_kernel, q_scale=q_scale),
        grid=(B * nsb,),
        in_specs=in_specs,
        out_specs=[head_spec, head_spec,
                   pl.BlockSpec((None, MLA_HEADS, None, MLA_VX, tm),
                                lambda i: (i // nsb, 0, (i % nsb) // per_kv, 0, (i % nsb) % per_kv))],
        out_shape=[jax.ShapeDtypeStruct((B, MLA_HEADS, S, MLA_HQ), BF16),
                   jax.ShapeDtypeStruct((B, MLA_HEADS, S, MLA_HQ), BF16),
                   jax.ShapeDtypeStruct((B, MLA_HEADS, S // tk, MLA_VX, tk), BF16)],
        compiler_params=_params(("parallel",)),
        name="mla_in",
    )(x2, g_mix, pos, freq, sign, w_mla, qn, w_uq, kvn, w_uk, w_vt, qhn, qhr, khn, khr)


def _gla_block(q_ref, k_ref, v_ref, la_ref, st_ref, emit, *, reverse, la_off, n_chunks):
    C = GLA_CHUNK
    row = lax.broadcasted_iota(jnp.int32, (C, C), 0)
    col = lax.broadcasted_iota(jnp.int32, (C, C), 1)
    if reverse:
        cum_mat = (col >= row).astype(BF16)
        att_mask = col > row
        last = 0
    else:
        cum_mat = (col <= row).astype(BF16)
        att_mask = col <= row
        last = C - 1
    order = range(n_chunks - 1, -1, -1) if reverse else range(n_chunks)
    for c in order:
        rows = slice(c * C, (c + 1) * C)
        la = la_ref[rows, la_off:la_off + GLA_HEADS * GLA_DK]
        la_hi = la.astype(BF16)
        la_lo = (la - la_hi.astype(F32)).astype(BF16)
        b_all = _dot(cum_mat, la_hi) + _dot(cum_mat, la_lo)
        for hd in range(GLA_HEADS):
            kcols = slice(hd * GLA_DK, (hd + 1) * GLA_DK)
            vcols = slice(hd * GLA_DV, (hd + 1) * GLA_DV)
            b = b_all[:, kcols]
            b_last = b[last:last + 1, :]
            q = q_ref[rows, kcols].astype(F32)
            k = k_ref[rows, kcols].astype(F32)
            v = v_ref[rows, vcols]
            q_t = (q * jnp.exp(b)).astype(BF16)
            k_t = (k * jnp.exp(-b)).astype(BF16)
            k_dec = (k * jnp.exp(b_last - b)).astype(BF16)
            att = jnp.where(att_mask, _dot_nt(q_t, k_t), 0.0).astype(BF16)
            st = st_ref[hd]
            o = _dot(att, v) + _dot_nt(q_t, st.astype(BF16))
            st_ref[hd] = st * jnp.exp(b_last) + _dot_tn(v, k_dec)
            emit(rows, vcols, o)


def _gla_fwd_kernel(q_ref, k_ref, v_ref, la_ref, o_ref, st_ref, *, n_chunks):
    @pl.when(pl.program_id(1) == 0)
    def _():
        st_ref[...] = jnp.zeros_like(st_ref)

    def emit(rows, vcols, o):
        o_ref[rows, vcols] = o

    _gla_block(q_ref, k_ref, v_ref, la_ref, st_ref, emit, reverse=False, la_off=0, n_chunks=n_chunks)


def _gla_bwd_kernel(q_ref, k_ref, v_ref, la_ref, of_ref, r_ref, gn_ref, o_ref, st_ref, *, n_chunks):
    @pl.when(pl.program_id(1) == 0)
    def _():
        st_ref[...] = jnp.zeros_like(st_ref)

    def emit(rows, vcols, o):
        tot = of_ref[rows, vcols] + o
        gate = r_ref[rows, vcols].astype(F32)
        o_ref[rows, vcols] = (_rms(tot, gn_ref[...]) * (gate * _sigmoid(gate))).astype(BF16)

    _gla_block(q_ref, k_ref, v_ref, la_ref, st_ref, emit, reverse=True,
               la_off=GLA_HEADS * GLA_DK, n_chunks=n_chunks)


def _gla(gq, gk, gv, la, gr, gnorm, B, S, tb):
    nb = S // tb
    nqk = GLA_HEADS * GLA_DK
    nv = GLA_HEADS * GLA_DV
    n_chunks = tb // GLA_CHUNK
    state = pltpu.VMEM((GLA_HEADS, GLA_DV, GLA_DK), F32)
    fwd = lambda n: pl.BlockSpec((tb, n), lambda b, i: (b * nb + i, 0))
    rev = lambda n: pl.BlockSpec((tb, n), lambda b, i: (b * nb + nb - 1 - i, 0))
    o_f = pl.pallas_call(
        functools.partial(_gla_fwd_kernel, n_chunks=n_chunks),
        grid=(B, nb),
        in_specs=[fwd(nqk), fwd(nqk), fwd(nv), fwd(2 * nqk)],
        out_specs=fwd(nv),
        out_shape=jax.ShapeDtypeStruct((B * S, nv), F32),
        scratch_shapes=[state],
        compiler_params=_params(("parallel", "arbitrary")),
        name="gla_fwd",
    )(gq, gk, gv, la)
    return pl.pallas_call(
        functools.partial(_gla_bwd_kernel, n_chunks=n_chunks),
        grid=(B, nb),
        in_specs=[rev(nqk), rev(nqk), rev(nv), rev(2 * nqk), rev(nv), rev(nv),
                  pl.BlockSpec(gnorm.shape, lambda b, i: (0, 0))],
        out_specs=rev(nv),
        out_shape=jax.ShapeDtypeStruct((B * S, nv), BF16),
        scratch_shapes=[state],
        compiler_params=_params(("parallel", "arbitrary")),
        name="gla_bwd",
    )(gq, gk, gv, la, o_f, gr, gnorm)


def _attn_kernel(q_ref, k_ref, vt_ref, o_ref, acc_ref, s_ref, p_ref, mx_ref, al_ref, *, tk, n_kv):
    q = q_ref[...]
    tq = q.shape[0]
    n_pairs = n_kv // 2

    def scores(pair, sw):
        for t in range(2):
            off = pl.multiple_of((2 * pair + t) * tk, tk)
            s = _dot_nt(k_ref[pl.ds(off, tk), :], q)
            s_ref[2 * sw + t] = s
            mx_ref[2 * sw + t] = jnp.max(s, axis=0, keepdims=True)

    def softmax(sw, m):
        for t in range(2):
            slot = 2 * sw + t
            m_new = jnp.maximum(m, mx_ref[slot])
            al_ref[slot] = jnp.exp2(m - m_new)
            p_ref[slot] = jnp.exp2(s_ref[slot] - m_new).astype(BF16)
            m = m_new
        return m

    def accumulate(pair, sw):
        for t in range(2):
            slot = 2 * sw + t
            acc_ref[...] = al_ref[slot] * acc_ref[...] + _dot(vt_ref[2 * pair + t], p_ref[slot])

    acc_ref[...] = jnp.zeros_like(acc_ref)
    scores(0, 0)
    scores(1, 1)
    m = softmax(0, jnp.full((1, tq), -1e30, F32))

    def body(it, m):
        pair = 1 + 2 * it
        scores(pair + 1, 0)
        m = softmax(1, m)
        accumulate(pair - 1, 0)
        scores(pair + 2, 1)
        m = softmax(0, m)
        accumulate(pair, 1)
        return m

    m = lax.fori_loop(0, (n_pairs - 2) // 2, body, m)
    softmax(1, m)
    accumulate(n_pairs - 2, 0)
    accumulate(n_pairs - 1, 1)
    inv_l = 1.0 / acc_ref[MLA_V:MLA_V + 1, :]
    o_ref[...] = (acc_ref[0:MLA_V, :] * inv_l).T.astype(BF16)


def _attn(q, k, vt, B, S, tq, tk):
    assert (S // tk) % 4 == 0, "kv tiles are consumed in pairs, two pairs per loop pass"
    return pl.pallas_call(
        functools.partial(_attn_kernel, tk=tk, n_kv=S // tk),
        grid=(B, MLA_HEADS, S // tq),
        in_specs=[pl.BlockSpec((None, None, tq, MLA_HQ), lambda b, h, i: (b, h, i, 0)),
                  pl.BlockSpec((None, None, S, MLA_HQ), lambda b, h, i: (b, h, 0, 0)),
                  pl.BlockSpec((None, None, S // tk, MLA_VX, tk), lambda b, h, i: (b, h, 0, 0, 0))],
        out_specs=pl.BlockSpec((None, tq, MLA_V), lambda b, h, i: (b, i, h)),
        out_shape=jax.ShapeDtypeStruct((B, S, MLA_HEADS * MLA_V), BF16),
        scratch_shapes=[pltpu.VMEM((MLA_VX, tq), F32), pltpu.VMEM((4, tk, tq), F32),
                        pltpu.VMEM((4, tk, tq), BF16), pltpu.VMEM((4, 1, tq), F32),
                        pltpu.VMEM((4, 1, tq), F32)],
        compiler_params=_params(("parallel", "parallel", "arbitrary")),
        name="mla_attn",
    )(q, k, vt)


def _post_mix_kernel(x_ref, oa_ref, ob_ref, gate_ref, wga_ref, wmo_ref, wo_ref, gf_ref, wrt_ref,
                     x1_ref, h2_ref, aff_ref):
    ya = _dot(oa_ref[...], wga_ref[...])
    yb = _dot(ob_ref[...], wmo_ref[...])
    ga = gate_ref[:, 0:D_MODEL].astype(F32)
    gb = gate_ref[:, D_MODEL:2 * D_MODEL].astype(F32)
    merged = (ga * ya + gb * yb).astype(BF16)
    x1 = x_ref[...] + _dot(merged, wo_ref[...])
    x1_ref[...] = x1
    h2 = _rms(x1, gf_ref[...]).astype(BF16)
    h2_ref[...] = h2
    logits = _dot_nt(wrt_ref[...], h2)
    e = jnp.exp(logits - jnp.max(logits, axis=0, keepdims=True))
    aff_ref[...] = e / jnp.sum(e, axis=0, keepdims=True)


def _post_mix(x2, oa, ob, gates, w_ga, w_mo, w_o, g_ffn, w_rt, B, S, tm):
    nsb = S // tm
    row = lambda n: pl.BlockSpec((tm, n), lambda i: (i, 0))
    return pl.pallas_call(
        _post_mix_kernel,
        grid=(B * nsb,),
        in_specs=[row(D_MODEL), row(D_MODEL), row(D_MODEL), row(2 * D_MODEL),
                  _const_spec(w_ga.shape), _const_spec(w_mo.shape), _const_spec(w_o.shape),
                  _const_spec(g_ffn.shape), _const_spec(w_rt.shape)],
        out_specs=[row(D_MODEL), row(D_MODEL),
                   pl.BlockSpec((None, N_EXPERTS, tm), lambda i: (i // nsb, 0, i % nsb))],
        out_shape=[jax.ShapeDtypeStruct((B * S, D_MODEL), F32),
                   jax.ShapeDtypeStruct((B * S, D_MODEL), BF16),
                   jax.ShapeDtypeStruct((B, N_EXPERTS, S), F32)],
        compiler_params=_params(("parallel",)),
        name="post_mix",
    )(x2, oa, ob, gates, w_ga, w_mo, w_o, g_ffn, w_rt)


def _select_kernel(aff_ref, pos_ref, pre_ref, *, cap, pblk):
    aff = aff_ref[...]
    S = aff.shape[1]
    bits = pltpu.bitcast(aff, jnp.int32)

    def search(i, thr):
        cand = thr | jnp.left_shift(jnp.int32(1), 30 - i)
        cnt = jnp.sum(jnp.where(bits >= cand, 1.0, 0.0), axis=1, keepdims=True)
        return jnp.where(cnt >= cap, cand, thr)

    thr = lax.fori_loop(0, 31, search, jnp.zeros((aff.shape[0], 1), jnp.int32))
    gt = bits > thr
    eq = bits == thr
    need = cap - jnp.sum(jnp.where(gt, 1.0, 0.0), axis=1, keepdims=True)

    r = lax.broadcasted_iota(jnp.int32, (pblk, pblk), 0)
    c = lax.broadcasted_iota(jnp.int32, (pblk, pblk), 1)
    before = (r < c).astype(BF16)

    def prefix_blocks(mask_of_block):
        carry = jnp.zeros((aff.shape[0], 1), F32)
        out = []
        for j in range(S // pblk):
            mk = mask_of_block(j)
            out.append(_dot(mk.astype(BF16), before) + carry)
            carry = carry + jnp.sum(mk, axis=1, keepdims=True)
        return out

    blk = lambda a, j: a[:, j * pblk:(j + 1) * pblk]
    tie_pre = prefix_blocks(lambda j: jnp.where(blk(eq, j), 1.0, 0.0))
    sel = [jnp.where(blk(gt, j) | (blk(eq, j) & (tie_pre[j] < need)), 1.0, 0.0)
           for j in range(S // pblk)]
    sel_pre = prefix_blocks(lambda j: sel[j])
    for j in range(S // pblk):
        pre_i = sel_pre[j].astype(jnp.int32)
        pre_ref[:, j * pblk:(j + 1) * pblk] = pre_i
        pos_ref[:, j * pblk:(j + 1) * pblk] = jnp.where(sel[j] > 0.5, pre_i, -1)


def _select(aff_t, cap, pblk):
    B, E, S = aff_t.shape
    spec = pl.BlockSpec((None, E, S), lambda b: (b, 0, 0))
    return pl.pallas_call(
        functools.partial(_select_kernel, cap=cap, pblk=pblk),
        grid=(B,),
        in_specs=[spec],
        out_specs=[spec, spec],
        out_shape=[jax.ShapeDtypeStruct((B, E, S), jnp.int32)] * 2,
        compiler_params=_params(("parallel",)),
        name="ec_select",
    )(aff_t)


def _ec_ffn_kernel(bs_ref, pos_ref, val_ref, h2_ref, wg_ref, wu_ref, wd_ref, x1_hbm,
                   out_ref, sem, *, n_parts, n_sub, tp, grp):
    b, part, e = pl.program_id(0), pl.program_id(1), pl.program_id(2)

    @pl.when(e == 0)
    def _():
        row0 = pl.multiple_of((b * n_parts + part) * tp, tp)
        cp = pltpu.make_async_copy(x1_hbm.at[pl.ds(row0, tp), :], out_ref, sem)
        cp.start()
        cp.wait()

    ts = tp // n_sub
    base = (b * N_EXPERTS + e) * (n_parts * n_sub + 1) + part * n_sub
    slot_iota = lax.broadcasted_iota(jnp.int32, (grp, ts), 0)

    for k in range(n_sub):
        s_lo = bs_ref[base + k]
        s_hi = bs_ref[base + k + 1]
        tok = slice(k * ts, (k + 1) * ts)

        def group(g, _, s_lo=s_lo, tok=tok):
            hit = pos_ref[:, tok] == (slot_iota + (s_lo + g * grp))
            xe = _dot(jnp.where(hit, 1.0, 0.0).astype(BF16), h2_ref[tok, :]).astype(BF16)
            a = _dot(xe, wg_ref[...])
            u = _dot(xe, wu_ref[...])
            mid = (a * _sigmoid(a) * u).astype(BF16)
            ye = _dot(mid, wd_ref[...]).astype(BF16)
            w = jnp.where(hit, val_ref[:, tok], 0.0).astype(BF16)
            out_ref[tok, :] += _dot_tn(w, ye)
            return 0

        lax.fori_loop(0, (s_hi - s_lo + grp - 1) // grp, group, 0)


def _ec_ffn(bs, pos4, val4, h2, wg, wu, wd, x1, B, S, n_parts, n_sub, grp):
    tp = S // n_parts
    row_spec = pl.BlockSpec((None, None, 1, tp), lambda b, p, e, bs_: (b, e, 0, p))
    w_spec = lambda shp: pl.BlockSpec((None,) + shp, lambda b, p, e, bs_: (e, 0, 0))
    part_spec = pl.BlockSpec((tp, D_MODEL), lambda b, p, e, bs_: (b * n_parts + p, 0))
    grid_spec = pltpu.PrefetchScalarGridSpec(
        num_scalar_prefetch=1,
        grid=(B, n_parts, N_EXPERTS),
        in_specs=[row_spec, row_spec, part_spec,
                  w_spec((D_MODEL, EXPERT_FF)), w_spec((D_MODEL, EXPERT_FF)),
                  w_spec((EXPERT_FF, D_MODEL)), pl.BlockSpec(memory_space=pl.ANY)],
        out_specs=part_spec,
        scratch_shapes=[pltpu.SemaphoreType.DMA(())],
    )
    return pl.pallas_call(
        functools.partial(_ec_ffn_kernel, n_parts=n_parts, n_sub=n_sub, tp=tp, grp=grp),
        grid_spec=grid_spec,
        out_shape=jax.ShapeDtypeStruct((B * S, D_MODEL), F32),
        compiler_params=_params(("parallel", "parallel", "arbitrary")),
        name="ec_ffn",
    )(bs, pos4, val4, h2, wg, wu, wd, x1)


def _tiles(S):
    return dict(tm=min(256, S), tb=min(256, S), tq=min(512, S), tk=min(512, S // 4),
                tmix=min(512, S), pblk=min(256, S),
                n_parts=max(1, min(4, S // 256)), n_sub=2, grp=160)


def kernel(x, positions, g_mix, w_in, gla_wa2_f, gla_ba_f, gla_wa2_b, gla_ba_b, gla_onorm,
           w_gla_out, mla_q_norm, w_uq, mla_kv_norm, w_ukv, q_head_norm, k_head_norm, w_mla_out,
           w_o, g_ffn, w_router, w_gate_e, w_up_e, w_down_e):
    B, S, D = x.shape
    assert D == D_MODEL and S % GLA_CHUNK == 0
    t = _tiles(S)
    T = B * S
    x2 = x.reshape(T, D)
    row = lambda v: v.reshape(1, -1).astype(F32)

    nqk, nv = GLA_HEADS * GLA_DK, GLA_HEADS * GLA_DV
    o_dec = 2 * nqk + 2 * nv
    o_mla = o_dec + 2 * GLA_GATE_RANK
    o_gate = o_mla + MLA_Q_RANK + MLA_KV_RANK + MLA_ROPE
    w_main = w_in[:, :o_dec].astype(BF16)
    w_dec = jnp.pad(w_in[:, o_dec:o_mla], ((0, 0), (0, LANES - 2 * GLA_GATE_RANK))).astype(BF16)
    w_mla = jnp.pad(w_in[:, o_mla:o_gate], ((0, 0), (0, MLA_ROPE))).astype(BF16)
    w_gates = w_in[:, o_gate:].astype(BF16)
    wa2 = jnp.zeros((LANES, 2 * nqk), F32)
    wa2 = wa2.at[0:GLA_GATE_RANK, 0:nqk].set(gla_wa2_f)
    wa2 = wa2.at[GLA_GATE_RANK:2 * GLA_GATE_RANK, nqk:].set(gla_wa2_b).astype(BF16)
    ba = jnp.concatenate([gla_ba_f, gla_ba_b]).reshape(1, -1)

    gq, gk, gv, gr, la, gates = _gla_in(x2, row(g_mix), w_main, w_dec, wa2, ba, w_gates, t["tm"])

    w_uq3 = w_uq.reshape(MLA_Q_RANK, MLA_HEADS, MLA_HQ)
    w_uq_r = jnp.concatenate([w_uq3[:, :, :MLA_NOPE].reshape(MLA_Q_RANK, -1),
                              w_uq3[:, :, MLA_NOPE:].reshape(MLA_Q_RANK, -1)], axis=1).astype(BF16)
    w_ukv3 = w_ukv.reshape(MLA_KV_RANK, MLA_HEADS, MLA_NOPE + MLA_V)
    w_uk = w_ukv3[:, :, :MLA_NOPE].reshape(MLA_KV_RANK, -1).astype(BF16)
    w_vt = w_ukv3[:, :, MLA_NOPE:].reshape(MLA_KV_RANK, -1).T.astype(BF16)
    half = MLA_ROPE // 2
    freqs = ROPE_THETA ** (-jnp.arange(half, dtype=F32) / half)
    freq128 = jnp.tile(freqs, LANES // half).reshape(1, LANES)
    sign128 = jnp.tile(jnp.concatenate([-jnp.ones(half, F32), jnp.ones(half, F32)]), 2).reshape(1, LANES)
    qhr = jnp.tile(q_head_norm[MLA_NOPE:], 2).reshape(1, LANES)
    khr = jnp.pad(k_head_norm[MLA_NOPE:], (0, MLA_ROPE)).reshape(1, LANES)
    q_scale = math.log2(math.e) / math.sqrt(MLA_HQ)
    q_b, k_b, v_t = _mla_in(x2, row(g_mix), positions.reshape(T, 1).astype(F32), freq128, sign128,
                            w_mla, row(mla_q_norm), w_uq_r, row(mla_kv_norm), w_uk, w_vt,
                            row(q_head_norm[:MLA_NOPE]), qhr, row(k_head_norm[:MLA_NOPE]), khr,
                            B, S, min(t["tm"], t["tk"]), t["tk"], q_scale)

    o_a = _gla(gq, gk, gv, la, gr, row(gla_onorm), B, S, t["tb"])
    o_b = _attn(q_b, k_b, v_t, B, S, t["tq"], t["tk"]).reshape(T, MLA_HEADS * MLA_V)

    x1, h2, aff_t = _post_mix(x2, o_a, o_b, gates, w_gla_out.astype(BF16), w_mla_out.astype(BF16),
                              w_o.astype(BF16), row(g_ffn), w_router.T.astype(BF16), B, S, t["tmix"])

    cap = EC_CAPACITY * S // N_EXPERTS
    pos, pre = _select(aff_t, cap, t["pblk"])
    sub_tokens = S // (t["n_parts"] * t["n_sub"])
    bs = jnp.concatenate([pre[:, :, ::sub_tokens], jnp.full((B, N_EXPERTS, 1), cap, jnp.int32)], axis=2)
    out = _ec_ffn(bs.reshape(-1), pos.reshape(B, N_EXPERTS, 1, S), aff_t.reshape(B, N_EXPERTS, 1, S),
                  h2, w_gate_e.astype(BF16), w_up_e.astype(BF16), w_down_e.astype(BF16), x1,
                  B, S, t["n_parts"], t["n_sub"], t["grp"])
    return out.reshape(B, S, D)
```

```python
import functools
import math

import jax
import jax.numpy as jnp
from jax import lax
from jax.experimental import pallas as pl
from jax.experimental.pallas import tpu as pltpu

D_MODEL = 1024
GLA_HEADS = 4
GLA_DK = 128
GLA_DV = 256
GLA_GATE_RANK = 16
GLA_TAU = 16.0
GLA_CHUNK = 64
MLA_HEADS = 8
MLA_NOPE = 128
MLA_ROPE = 64
MLA_V = 128
MLA_Q_RANK = 384
MLA_KV_RANK = 256
MLA_HQ = MLA_NOPE + MLA_ROPE
MLA_VX = MLA_V + 16
ROPE_THETA = 10000.0
N_EXPERTS = 16
EC_CAPACITY = 2
EXPERT_FF = 1024
EPS = 1e-6

LANES = 128
VMEM_LIMIT = 52 * 1024 * 1024

BF16 = jnp.bfloat16
F32 = jnp.float32


def _dot(a, b):
    return jnp.dot(a, b, preferred_element_type=F32)


def _dot_nt(a, b):
    return lax.dot_general(a, b, (((1,), (1,)), ((), ())), preferred_element_type=F32)


def _dot_tn(a, b):
    return lax.dot_general(a, b, (((0,), (0,)), ((), ())), preferred_element_type=F32)


def _rms(x, gain):
    ms = jnp.mean(x * x, axis=-1, keepdims=True)
    return x * lax.rsqrt(ms + EPS) * gain


def _sigmoid(x):
    return 0.5 * jnp.tanh(0.5 * x) + 0.5


def _const_spec(shape):
    nd = len(shape)
    return pl.BlockSpec(shape, lambda *_: (0,) * nd, pipeline_mode=pl.Buffered(1))


def _params(sem):
    return pltpu.CompilerParams(dimension_semantics=sem, vmem_limit_bytes=VMEM_LIMIT)


def _gla_in_kernel(x_ref, g_ref, wmain_ref, wdec_ref, wa2_ref, ba_ref, wgate_ref,
                   q_ref, k_ref, v_ref, r_ref, la_ref, gate_ref):
    h = _rms(x_ref[...], g_ref[...]).astype(BF16)
    nqk = GLA_HEADS * GLA_DK
    nv = GLA_HEADS * GLA_DV
    dec = _dot(h, wdec_ref[...]).astype(BF16)
    q_ref[...] = (_dot(h, wmain_ref[:, 0:nqk]) * (GLA_DK ** -0.5)).astype(BF16)
    k_ref[...] = _dot(h, wmain_ref[:, nqk:2 * nqk]).astype(BF16)
    z = _dot(dec, wa2_ref[...]) + ba_ref[...]
    log_sig = jnp.minimum(z, 0.0) - jnp.log1p(jnp.exp(-jnp.abs(z)))
    la_ref[...] = log_sig * (1.0 / GLA_TAU)
    v_ref[...] = _dot(h, wmain_ref[:, 2 * nqk:2 * nqk + nv]).astype(BF16)
    r_ref[...] = _dot(h, wmain_ref[:, 2 * nqk + nv:2 * nqk + 2 * nv]).astype(BF16)
    gate_ref[...] = _sigmoid(_dot(h, wgate_ref[...])).astype(BF16)


def _gla_in(x2, g_mix, w_main, w_dec, wa2, ba, w_gates, tm):
    T = x2.shape[0]
    nqk = GLA_HEADS * GLA_DK
    nv = GLA_HEADS * GLA_DV
    row = lambda n: pl.BlockSpec((tm, n), lambda i: (i, 0))
    return pl.pallas_call(
        _gla_in_kernel,
        grid=(T // tm,),
        in_specs=[row(D_MODEL), _const_spec(g_mix.shape), _const_spec(w_main.shape),
                  _const_spec(w_dec.shape), _const_spec(wa2.shape), _const_spec(ba.shape),
                  _const_spec(w_gates.shape)],
        out_specs=[row(nqk), row(nqk), row(nv), row(nv), row(2 * nqk), row(2 * D_MODEL)],
        out_shape=[jax.ShapeDtypeStruct((T, nqk), BF16), jax.ShapeDtypeStruct((T, nqk), BF16),
                   jax.ShapeDtypeStruct((T, nv), BF16), jax.ShapeDtypeStruct((T, nv), BF16),
                   jax.ShapeDtypeStruct((T, 2 * nqk), F32),
                   jax.ShapeDtypeStruct((T, 2 * D_MODEL), BF16)],
        compiler_params=_params(("parallel",)),
        name="gla_in",
    )(x2, g_mix, w_main, w_dec, wa2, ba, w_gates)


def _rope128(x, cos, sin_signed):
    lane = lax.broadcasted_iota(jnp.int32, x.shape, 1)
    first_half = (lane % MLA_ROPE) < (MLA_ROPE // 2)
    partner = jnp.where(first_half,
                        pltpu.roll(x, LANES - MLA_ROPE // 2, 1),
                        pltpu.roll(x, MLA_ROPE // 2, 1))
    return x * cos + partner * sin_signed


def _mla_in_kernel(x_ref, g_ref, pos_ref, freq_ref, sign_ref, wmla_ref, qn_ref, wuq_ref,
                   kvn_ref, wuk_ref, wvt_ref, qhn_ref, qhr_ref, khn_ref, khr_ref,
                   q_ref, k_ref, vt_ref, *, q_scale):
    h = _rms(x_ref[...], g_ref[...]).astype(BF16)
    lat = _dot(h, wmla_ref[...])
    cq = lat[:, 0:MLA_Q_RANK]
    ckv = lat[:, MLA_Q_RANK:MLA_Q_RANK + MLA_KV_RANK]
    kr2 = lat[:, MLA_Q_RANK + MLA_KV_RANK:MLA_Q_RANK + MLA_KV_RANK + LANES]

    ang = pos_ref[...] * freq_ref[...]
    cos = jnp.cos(ang)
    sin_signed = jnp.sin(ang) * sign_ref[...]
    lane = lax.broadcasted_iota(jnp.int32, ang.shape, 1)
    low = lane < MLA_ROPE

    nn = MLA_HEADS * MLA_NOPE
    qf = _dot(_rms(cq, qn_ref[...]).astype(BF16), wuq_ref[...])
    ckvn = _rms(ckv, kvn_ref[...]).astype(BF16)
    kn = _dot(ckvn, wuk_ref[...])
    vt = _dot_nt(wvt_ref[...], ckvn).astype(BF16)
    pad_row = lax.broadcasted_iota(jnp.int32, (MLA_VX - MLA_V, vt.shape[1]), 0)
    ones_then_zeros = jnp.where(pad_row == 0, 1.0, 0.0).astype(BF16)
    for hd in range(MLA_HEADS):
        vt_ref[hd, 0:MLA_V, :] = vt[hd * MLA_V:(hd + 1) * MLA_V, :]
        vt_ref[hd, MLA_V:MLA_VX, :] = ones_then_zeros

    ss_kr = jnp.sum(kr2 * kr2, axis=-1, keepdims=True)
    kr_rot = _rope128(kr2 * khr_ref[...], cos, sin_signed)[:, 0:MLA_ROPE]

    for pair in range(MLA_HEADS // 2):
        xr = qf[:, nn + pair * LANES: nn + (pair + 1) * LANES]
        sq = xr * xr
        ss_lo = jnp.sum(jnp.where(low, sq, 0.0), axis=-1, keepdims=True)
        ss_hi = jnp.sum(sq, axis=-1, keepdims=True) - ss_lo
        rstd = []
        for j, ss_r in enumerate((ss_lo, ss_hi)):
            hd = 2 * pair + j
            xn = qf[:, hd * MLA_NOPE:(hd + 1) * MLA_NOPE]
            ss = jnp.sum(xn * xn, axis=-1, keepdims=True) + ss_r
            r = lax.rsqrt(ss * (1.0 / MLA_HQ) + EPS)
            rstd.append(r)
            q_ref[hd, :, 0:MLA_NOPE] = (xn * r * (qhn_ref[...] * q_scale)).astype(BF16)
        scaled = xr * jnp.where(low, rstd[0], rstd[1]) * (qhr_ref[...] * q_scale)
        rot = _rope128(scaled, cos, sin_signed).astype(BF16)
        q_ref[2 * pair, :, MLA_NOPE:MLA_HQ] = rot[:, 0:MLA_ROPE]
        q_ref[2 * pair + 1, :, MLA_NOPE:MLA_HQ] = rot[:, MLA_ROPE:LANES]

    for hd in range(MLA_HEADS):
        xn = kn[:, hd * MLA_NOPE:(hd + 1) * MLA_NOPE]
        ss = jnp.sum(xn * xn, axis=-1, keepdims=True) + ss_kr
        r = lax.rsqrt(ss * (1.0 / MLA_HQ) + EPS)
        k_ref[hd, :, 0:MLA_NOPE] = (xn * r * khn_ref[...]).astype(BF16)
        k_ref[hd, :, MLA_NOPE:MLA_HQ] = (kr_rot * r).astype(BF16)


def _mla_in(x2, g_mix, pos, freq, sign, w_mla, qn, w_uq, kvn, w_uk, w_vt, qhn, qhr, khn, khr,
            B, S, tm, tk, q_scale):
    nsb = S // tm
    per_kv = tk // tm
    assert tk % tm == 0
    consts = (g_mix, None, freq, sign, w_mla, qn, w_uq, kvn, w_uk, w_vt, qhn, qhr, khn, khr)
    in_specs = [pl.BlockSpec((tm, D_MODEL), lambda i: (i, 0))]
    for c in consts:
        in_specs.append(pl.BlockSpec((tm, 1), lambda i: (i, 0)) if c is None else _const_spec(c.shape))
    head_spec = pl.BlockSpec((None, MLA_HEADS, tm, MLA_HQ), lambda i: (i // nsb, 0, i % nsb, 0))
    return pl.pallas_call(
        functools.partial(_mla_in_kernel, q_scale=q_scale),
        grid=(B * nsb,),
        in_specs=in_specs,
        out_specs=[head_spec, head_spec,
                   pl.BlockSpec((None, MLA_HEADS, None, MLA_VX, tm),
                                lambda i: (i // nsb, 0, (i % nsb) // per_kv, 0, (i % nsb) % per_kv))],
        out_shape=[jax.ShapeDtypeStruct((B, MLA_HEADS, S, MLA_HQ), BF16),
                   jax.ShapeDtypeStruct((B, MLA_HEADS, S, MLA_HQ), BF16),
                   jax.ShapeDtypeStruct((B, MLA_HEADS, S // tk, MLA_VX, tk), BF16)],
        compiler_params=_params(("parallel",)),
        name="mla_in",
    )(x2, g_mix, pos, freq, sign, w_mla, qn, w_uq, kvn, w_uk, w_vt, qhn, qhr, khn, khr)


def _gla_block(q_ref, k_ref, v_ref, la_ref, st_ref, emit, *, reverse, la_off, n_chunks):
    C = GLA_CHUNK
    row = lax.broadcasted_iota(jnp.int32, (C, C), 0)
    col = lax.broadcasted_iota(jnp.int32, (C, C), 1)
    if reverse:
        cum_mat = (col >= row).astype(BF16)
        att_mask = col > row
        last = 0
    else:
        cum_mat = (col <= row).astype(BF16)
        att_mask = col <= row
        last = C - 1
    order = range(n_chunks - 1, -1, -1) if reverse else range(n_chunks)
    for c in order:
        rows = slice(c * C, (c + 1) * C)
        la = la_ref[rows, la_off:la_off + GLA_HEADS * GLA_DK]
        la_hi = la.astype(BF16)
        la_lo = (la - la_hi.astype(F32)).astype(BF16)
        b_all = _dot(cum_mat, la_hi) + _dot(cum_mat, la_lo)
        for hd in range(GLA_HEADS):
            kcols = slice(hd * GLA_DK, (hd + 1) * GLA_DK)
            vcols = slice(hd * GLA_DV, (hd + 1) * GLA_DV)
            b = b_all[:, kcols]
            b_last = b[last:last + 1, :]
            q = q_ref[rows, kcols].astype(F32)
            k = k_ref[rows, kcols].astype(F32)
            v = v_ref[rows, vcols]
            q_t = (q * jnp.exp(b)).astype(BF16)
            k_t = (k * jnp.exp(-b)).astype(BF16)
            k_dec = (k * jnp.exp(b_last - b)).astype(BF16)
            att = jnp.where(att_mask, _dot_nt(q_t, k_t), 0.0).astype(BF16)
            st = st_ref[hd]
            o = _dot(att, v) + _dot_nt(q_t, st.astype(BF16))
            st_ref[hd] = st * jnp.exp(b_last) + _dot_tn(v, k_dec)
            emit(rows, vcols, o)


def _gla_fwd_kernel(q_ref, k_ref, v_ref, la_ref, o_ref, st_ref, *, n_chunks):
    @pl.when(pl.program_id(1) == 0)
    def _():
        st_ref[...] = jnp.zeros_like(st_ref)

    def emit(rows, vcols, o):
        o_ref[rows, vcols] = o

    _gla_block(q_ref, k_ref, v_ref, la_ref, st_ref, emit, reverse=False, la_off=0, n_chunks=n_chunks)


def _gla_bwd_kernel(q_ref, k_ref, v_ref, la_ref, of_ref, r_ref, gn_ref, o_ref, st_ref, *, n_chunks):
    @pl.when(pl.program_id(1) == 0)
    def _():
        st_ref[...] = jnp.zeros_like(st_ref)

    def emit(rows, vcols, o):
        tot = of_ref[rows, vcols] + o
        gate = r_ref[rows, vcols].astype(F32)
        o_ref[rows, vcols] = (_rms(tot, gn_ref[...]) * (gate * _sigmoid(gate))).astype(BF16)

    _gla_block(q_ref, k_ref, v_ref, la_ref, st_ref, emit, reverse=True,
               la_off=GLA_HEADS * GLA_DK, n_chunks=n_chunks)


def _gla(gq, gk, gv, la, gr, gnorm, B, S, tb):
    nb = S // tb
    nqk = GLA_HEADS * GLA_DK
    nv = GLA_HEADS * GLA_DV
    n_chunks = tb // GLA_CHUNK
    state = pltpu.VMEM((GLA_HEADS, GLA_DV, GLA_DK), F32)
    fwd = lambda n: pl.BlockSpec((tb, n), lambda b, i: (b * nb + i, 0))
    rev = lambda n: pl.BlockSpec((tb, n), lambda b, i: (b * nb + nb - 1 - i, 0))
    o_f = pl.pallas_call(
        functools.partial(_gla_fwd_kernel, n_chunks=n_chunks),
        grid=(B, nb),
        in_specs=[fwd(nqk), fwd(nqk), fwd(nv), fwd(2 * nqk)],
        out_specs=fwd(nv),
        out_shape=jax.ShapeDtypeStruct((B * S, nv), F32),
        scratch_shapes=[state],
        compiler_params=_params(("parallel", "arbitrary")),
        name="gla_fwd",
    )(gq, gk, gv, la)
    return pl.pallas_call(
        functools.partial(_gla_bwd_kernel, n_chunks=n_chunks),
        grid=(B, nb),
        in_specs=[rev(nqk), rev(nqk), rev(nv), rev(2 * nqk), rev(nv), rev(nv),
                  pl.BlockSpec(gnorm.shape, lambda b, i: (0, 0))],
        out_specs=rev(nv),
        out_shape=jax.ShapeDtypeStruct((B * S, nv), BF16),
        scratch_shapes=[state],
        compiler_params=_params(("parallel", "arbitrary")),
        name="gla_bwd",
    )(gq, gk, gv, la, o_f, gr, gnorm)


ATTN_REF_MAX = 48.0


def _attn_kernel(qmax_ref, q_ref, k_ref, vt_ref, o_ref, acc_ref, s_ref, p_ref, mx_ref, al_ref, ref_smem,
                 *, tk, n_kv):
    q = q_ref[...]
    tq = q.shape[0]
    n_pairs = n_kv // 2

    @pl.when(pl.program_id(2) == 0)
    def _():
        kk = k_ref[...].astype(F32)
        k_sq = _dot_nt(jnp.ones((8, MLA_HQ), BF16), (kk * kk).astype(BF16))
        k_max = jnp.sqrt(jnp.max(k_sq, axis=1, keepdims=True))
        ref_smem[0] = qmax_ref[0, 0] * k_max[0, 0]

    ref = ref_smem[0]

    def k_tile(pair, t):
        off = (2 * pair + t) * tk
        return k_ref[pl.ds(off if isinstance(off, int) else pl.multiple_of(off, tk), tk), :]

    def finish():
        inv_l = 1.0 / acc_ref[MLA_V:MLA_V + 1, :]
        o_ref[...] = (acc_ref[0:MLA_V, :] * inv_l).T.astype(BF16)

    @pl.when(ref <= ATTN_REF_MAX)
    def _bounded():
        def scores_exp(pair, sw):
            for t in range(2):
                p_ref[2 * sw + t] = jnp.exp2(_dot_nt(k_tile(pair, t), q) - ref).astype(BF16)

        def accumulate(pair, sw):
            p2 = p_ref[2 * sw:2 * sw + 2].reshape(2 * tk, tq)
            acc_ref[...] += _dot(vt_ref[pair], p2)

        acc_ref[...] = jnp.zeros_like(acc_ref)
        scores_exp(0, 0)
        for pair in range(1, n_pairs):
            scores_exp(pair, pair % 2)
            accumulate(pair - 1, (pair - 1) % 2)
        accumulate(n_pairs - 1, (n_pairs - 1) % 2)
        finish()

    @pl.when(ref > ATTN_REF_MAX)
    def _online():
        def scores(pair, sw):
            for t in range(2):
                s = _dot_nt(k_tile(pair, t), q)
                s_ref[2 * sw + t] = s
                mx_ref[2 * sw + t] = jnp.max(s, axis=0, keepdims=True)

        def softmax(sw, m):
            for t in range(2):
                slot = 2 * sw + t
                m_new = jnp.maximum(m, mx_ref[slot])
                al_ref[slot] = jnp.exp2(m - m_new)
                p_ref[slot] = jnp.exp2(s_ref[slot] - m_new).astype(BF16)
                m = m_new
            return m

        def accumulate(pair, sw):
            for t in range(2):
                slot = 2 * sw + t
                acc_ref[...] = (al_ref[slot] * acc_ref[...]
                                + _dot(vt_ref[pair, :, t * tk:(t + 1) * tk], p_ref[slot]))

        acc_ref[...] = jnp.zeros_like(acc_ref)
        scores(0, 0)
        scores(1, 1)
        m = softmax(0, jnp.full((1, tq), -1e30, F32))

        def body(it, m):
            pair = 1 + 2 * it
            scores(pair + 1, 0)
            m = softmax(1, m)
            accumulate(pair - 1, 0)
            scores(pair + 2, 1)
            m = softmax(0, m)
            accumulate(pair, 1)
            return m

        m = lax.fori_loop(0, (n_pairs - 2) // 2, body, m)
        softmax(1, m)
        accumulate(n_pairs - 2, 0)
        accumulate(n_pairs - 1, 1)
        finish()


def _attn(qmax, q, k, vt, B, S, tq, tk):
    assert (S // tk) % 4 == 0, "kv tiles are consumed in pairs, two pairs per loop pass"
    return pl.pallas_call(
        functools.partial(_attn_kernel, tk=tk, n_kv=S // tk),
        grid=(B, MLA_HEADS, S // tq),
        in_specs=[pl.BlockSpec(memory_space=pltpu.SMEM),
                  pl.BlockSpec((None, None, tq, MLA_HQ), lambda b, h, i: (b, h, i, 0)),
                  pl.BlockSpec((None, None, S, MLA_HQ), lambda b, h, i: (b, h, 0, 0)),
                  pl.BlockSpec((None, None, S // (2 * tk), MLA_VX, 2 * tk), lambda b, h, i: (b, h, 0, 0, 0))],
        out_specs=pl.BlockSpec((None, tq, MLA_V), lambda b, h, i: (b, i, h)),
        out_shape=jax.ShapeDtypeStruct((B, S, MLA_HEADS * MLA_V), BF16),
        scratch_shapes=[pltpu.VMEM((MLA_VX, tq), F32), pltpu.VMEM((4, tk, tq), F32),
                        pltpu.VMEM((4, tk, tq), BF16), pltpu.VMEM((4, 1, tq), F32),
                        pltpu.VMEM((4, 1, tq), F32), pltpu.SMEM((1,), F32)],
        compiler_params=_params(("parallel", "parallel", "arbitrary")),
        name="mla_attn",
    )(qmax, q, k, vt)


def _post_mix_kernel(x_ref, oa_ref, ob_ref, gate_ref, wga_ref, wmo_ref, wo_ref, gf_ref, wrt_ref,
                     x1_ref, h2_ref, aff_ref):
    ya = _dot(oa_ref[...], wga_ref[...])
    yb = _dot(ob_ref[...], wmo_ref[...])
    ga = gate_ref[:, 0:D_MODEL].astype(F32)
    gb = gate_ref[:, D_MODEL:2 * D_MODEL].astype(F32)
    merged = (ga * ya + gb * yb).astype(BF16)
    x1 = x_ref[...] + _dot(merged, wo_ref[...])
    x1_ref[...] = x1
    h2 = _rms(x1, gf_ref[...]).astype(BF16)
    h2_ref[...] = h2
    logits = _dot_nt(wrt_ref[...], h2)
    e = jnp.exp(logits - jnp.max(logits, axis=0, keepdims=True))
    aff_ref[...] = e / jnp.sum(e, axis=0, keepdims=True)


def _post_mix(x2, oa, ob, gates, w_ga, w_mo, w_o, g_ffn, w_rt, B, S, tm):
    nsb = S // tm
    row = lambda n: pl.BlockSpec((tm, n), lambda i: (i, 0))
    return pl.pallas_call(
        _post_mix_kernel,
        grid=(B * nsb,),
        in_specs=[row(D_MODEL), row(D_MODEL), row(D_MODEL), row(2 * D_MODEL),
                  _const_spec(w_ga.shape), _const_spec(w_mo.shape), _const_spec(w_o.shape),
                  _const_spec(g_ffn.shape), _const_spec(w_rt.shape)],
        out_specs=[row(D_MODEL), row(D_MODEL),
                   pl.BlockSpec((None, N_EXPERTS, tm), lambda i: (i // nsb, 0, i % nsb))],
        out_shape=[jax.ShapeDtypeStruct((B * S, D_MODEL), F32),
                   jax.ShapeDtypeStruct((B * S, D_MODEL), BF16),
                   jax.ShapeDtypeStruct((B, N_EXPERTS, S), F32)],
        compiler_params=_params(("parallel",)),
        name="post_mix",
    )(x2, oa, ob, gates, w_ga, w_mo, w_o, g_ffn, w_rt)


def _select_kernel(aff_ref, pos_ref, pre_ref, *, cap, pblk):
    aff = aff_ref[...]
    S = aff.shape[1]
    bits = pltpu.bitcast(aff, jnp.int32)

    def search(i, thr):
        cand = thr | jnp.left_shift(jnp.int32(1), 30 - i)
        cnt = jnp.sum(jnp.where(bits >= cand, 1.0, 0.0), axis=1, keepdims=True)
        return jnp.where(cnt >= cap, cand, thr)

    thr = lax.fori_loop(0, 31, search, jnp.zeros((aff.shape[0], 1), jnp.int32))
    gt = bits > thr
    eq = bits == thr
    need = cap - jnp.sum(jnp.where(gt, 1.0, 0.0), axis=1, keepdims=True)

    r = lax.broadcasted_iota(jnp.int32, (pblk, pblk), 0)
    c = lax.broadcasted_iota(jnp.int32, (pblk, pblk), 1)
    before = (r < c).astype(BF16)

    def prefix_blocks(mask_of_block):
        carry = jnp.zeros((aff.shape[0], 1), F32)
        out = []
        for j in range(S // pblk):
            mk = mask_of_block(j)
            out.append(_dot(mk.astype(BF16), before) + carry)
            carry = carry + jnp.sum(mk, axis=1, keepdims=True)
        return out

    blk = lambda a, j: a[:, j * pblk:(j + 1) * pblk]
    tie_pre = prefix_blocks(lambda j: jnp.where(blk(eq, j), 1.0, 0.0))
    sel = [jnp.where(blk(gt, j) | (blk(eq, j) & (tie_pre[j] < need)), 1.0, 0.0)
           for j in range(S // pblk)]
    sel_pre = prefix_blocks(lambda j: sel[j])
    for j in range(S // pblk):
        pre_i = sel_pre[j].astype(jnp.int32)
        pre_ref[:, j * pblk:(j + 1) * pblk] = pre_i
        pos_ref[:, j * pblk:(j + 1) * pblk] = jnp.where(sel[j] > 0.5, pre_i, -1)


def _select(aff_t, cap, pblk):
    B, E, S = aff_t.shape
    spec = pl.BlockSpec((None, E, S), lambda b: (b, 0, 0))
    return pl.pallas_call(
        functools.partial(_select_kernel, cap=cap, pblk=pblk),
        grid=(B,),
        in_specs=[spec],
        out_specs=[spec, spec],
        out_shape=[jax.ShapeDtypeStruct((B, E, S), jnp.int32)] * 2,
        compiler_params=_params(("parallel",)),
        name="ec_select",
    )(aff_t)


def _ec_ffn_kernel(bs_ref, pos_ref, val_ref, h2_ref, wg_ref, wu_ref, wd_ref, x1_hbm,
                   out_ref, sem, *, n_parts, n_sub, tp, grp):
    b, part, e = pl.program_id(0), pl.program_id(1), pl.program_id(2)

    @pl.when(e == 0)
    def _():
        row0 = pl.multiple_of((b * n_parts + part) * tp, tp)
        cp = pltpu.make_async_copy(x1_hbm.at[pl.ds(row0, tp), :], out_ref, sem)
        cp.start()
        cp.wait()

    ts = tp // n_sub
    base = (b * N_EXPERTS + e) * (n_parts * n_sub + 1) + part * n_sub
    slot_iota = lax.broadcasted_iota(jnp.int32, (grp, ts), 0)

    for k in range(n_sub):
        s_lo = bs_ref[base + k]
        s_hi = bs_ref[base + k + 1]
        tok = slice(k * ts, (k + 1) * ts)

        def group(g, _, s_lo=s_lo, tok=tok):
            hit = pos_ref[:, tok] == (slot_iota + (s_lo + g * grp))
            xe = _dot(jnp.where(hit, 1.0, 0.0).astype(BF16), h2_ref[tok, :]).astype(BF16)
            a = _dot(xe, wg_ref[...])
            u = _dot(xe, wu_ref[...])
            mid = (a * _sigmoid(a) * u).astype(BF16)
            ye = _dot(mid, wd_ref[...]).astype(BF16)
            w = jnp.where(hit, val_ref[:, tok], 0.0).astype(BF16)
            out_ref[tok, :] += _dot_tn(w, ye)
            return 0

        lax.fori_loop(0, (s_hi - s_lo + grp - 1) // grp, group, 0)


def _ec_ffn(bs, pos4, val4, h2, wg, wu, wd, x1, B, S, n_parts, n_sub, grp):
    tp = S // n_parts
    row_spec = pl.BlockSpec((None, None, 1, tp), lambda b, p, e, bs_: (b, e, 0, p))
    w_spec = lambda shp: pl.BlockSpec((None,) + shp, lambda b, p, e, bs_: (e, 0, 0))
    part_spec = pl.BlockSpec((tp, D_MODEL), lambda b, p, e, bs_: (b * n_parts + p, 0))
    grid_spec = pltpu.PrefetchScalarGridSpec(
        num_scalar_prefetch=1,
        grid=(B, n_parts, N_EXPERTS),
        in_specs=[row_spec, row_spec, part_spec,
                  w_spec((D_MODEL, EXPERT_FF)), w_spec((D_MODEL, EXPERT_FF)),
                  w_spec((EXPERT_FF, D_MODEL)), pl.BlockSpec(memory_space=pl.ANY)],
        out_specs=part_spec,
        scratch_shapes=[pltpu.SemaphoreType.DMA(())],
    )
    return pl.pallas_call(
        functools.partial(_ec_ffn_kernel, n_parts=n_parts, n_sub=n_sub, tp=tp, grp=grp),
        grid_spec=grid_spec,
        out_shape=jax.ShapeDtypeStruct((B * S, D_MODEL), F32),
        compiler_params=_params(("parallel", "parallel", "arbitrary")),
        name="ec_ffn",
    )(bs, pos4, val4, h2, wg, wu, wd, x1)


def _tiles(S):
    return dict(tm=min(256, S), tb=min(256, S), tq=min(512, S), tk=min(512, S // 4),
                tmix=min(512, S), pblk=min(256, S),
                n_parts=max(1, min(4, S // 256)), n_sub=2, grp=160)


def kernel(x, positions, g_mix, w_in, gla_wa2_f, gla_ba_f, gla_wa2_b, gla_ba_b, gla_onorm,
           w_gla_out, mla_q_norm, w_uq, mla_kv_norm, w_ukv, q_head_norm, k_head_norm, w_mla_out,
           w_o, g_ffn, w_router, w_gate_e, w_up_e, w_down_e):
    B, S, D = x.shape
    assert D == D_MODEL and S % GLA_CHUNK == 0
    t = _tiles(S)
    T = B * S
    x2 = x.reshape(T, D)
    row = lambda v: v.reshape(1, -1).astype(F32)

    nqk, nv = GLA_HEADS * GLA_DK, GLA_HEADS * GLA_DV
    o_dec = 2 * nqk + 2 * nv
    o_mla = o_dec + 2 * GLA_GATE_RANK
    o_gate = o_mla + MLA_Q_RANK + MLA_KV_RANK + MLA_ROPE
    w_main = w_in[:, :o_dec].astype(BF16)
    w_dec = jnp.pad(w_in[:, o_dec:o_mla], ((0, 0), (0, LANES - 2 * GLA_GATE_RANK))).astype(BF16)
    w_mla = jnp.pad(w_in[:, o_mla:o_gate], ((0, 0), (0, MLA_ROPE))).astype(BF16)
    w_gates = w_in[:, o_gate:].astype(BF16)
    wa2 = jnp.zeros((LANES, 2 * nqk), F32)
    wa2 = wa2.at[0:GLA_GATE_RANK, 0:nqk].set(gla_wa2_f)
    wa2 = wa2.at[GLA_GATE_RANK:2 * GLA_GATE_RANK, nqk:].set(gla_wa2_b).astype(BF16)
    ba = jnp.concatenate([gla_ba_f, gla_ba_b]).reshape(1, -1)

    gq, gk, gv, gr, la, gates = _gla_in(x2, row(g_mix), w_main, w_dec, wa2, ba, w_gates, t["tm"])

    w_uq3 = w_uq.reshape(MLA_Q_RANK, MLA_HEADS, MLA_HQ)
    w_uq_r = jnp.concatenate([w_uq3[:, :, :MLA_NOPE].reshape(MLA_Q_RANK, -1),
                              w_uq3[:, :, MLA_NOPE:].reshape(MLA_Q_RANK, -1)], axis=1).astype(BF16)
    w_ukv3 = w_ukv.reshape(MLA_KV_RANK, MLA_HEADS, MLA_NOPE + MLA_V)
    w_uk = w_ukv3[:, :, :MLA_NOPE].reshape(MLA_KV_RANK, -1).astype(BF16)
    w_vt = w_ukv3[:, :, MLA_NOPE:].reshape(MLA_KV_RANK, -1).T.astype(BF16)
    half = MLA_ROPE // 2
    freqs = ROPE_THETA ** (-jnp.arange(half, dtype=F32) / half)
    freq128 = jnp.tile(freqs, LANES // half).reshape(1, LANES)
    sign128 = jnp.tile(jnp.concatenate([-jnp.ones(half, F32), jnp.ones(half, F32)]), 2).reshape(1, LANES)
    qhr = jnp.tile(q_head_norm[MLA_NOPE:], 2).reshape(1, LANES)
    khr = jnp.pad(k_head_norm[MLA_NOPE:], (0, MLA_ROPE)).reshape(1, LANES)
    q_scale = math.log2(math.e) / math.sqrt(MLA_HQ)
    q_b, k_b, v_t = _mla_in(x2, row(g_mix), positions.reshape(T, 1).astype(F32), freq128, sign128,
                            w_mla, row(mla_q_norm), w_uq_r, row(mla_kv_norm), w_uk, w_vt,
                            row(q_head_norm[:MLA_NOPE]), qhr, row(k_head_norm[:MLA_NOPE]), khr,
                            B, S, min(t["tm"], t["tk"]), 2 * t["tk"], q_scale)

    o_a = _gla(gq, gk, gv, la, gr, row(gla_onorm), B, S, t["tb"])
    q_bound = (jnp.max(jnp.abs(q_head_norm)) * (math.sqrt(MLA_HQ) * q_scale)).reshape(1, 1).astype(F32)
    o_b = _attn(q_bound, q_b, k_b, v_t, B, S, t["tq"], t["tk"]).reshape(T, MLA_HEADS * MLA_V)

    x1, h2, aff_t = _post_mix(x2, o_a, o_b, gates, w_gla_out.astype(BF16), w_mla_out.astype(BF16),
                              w_o.astype(BF16), row(g_ffn), w_router.T.astype(BF16), B, S, t["tmix"])

    cap = EC_CAPACITY * S // N_EXPERTS
    pos, pre = _select(aff_t, cap, t["pblk"])
    sub_tokens = S // (t["n_parts"] * t["n_sub"])
    bs = jnp.concatenate([pre[:, :, ::sub_tokens], jnp.full((B, N_EXPERTS, 1), cap, jnp.int32)], axis=2)
    out = _ec_ffn(bs.reshape(-1), pos.reshape(B, N_EXPERTS, 1, S), aff_t.reshape(B, N_EXPERTS, 1, S),
                  h2, w_gate_e.astype(BF16), w_up_e.astype(BF16), w_down_e.astype(BF16), x1,
                  B, S, t["n_parts"], t["n_sub"], t["grp"])
    return out.reshape(B, S, D)
```

```python
import functools
import math

import jax
import jax.numpy as jnp
from jax import lax
from jax.experimental import pallas as pl
from jax.experimental.pallas import tpu as pltpu

D_MODEL = 1024
GLA_HEADS = 4
GLA_DK = 128
GLA_DV = 256
GLA_GATE_RANK = 16
GLA_TAU = 16.0
GLA_CHUNK = 64
MLA_HEADS = 8
MLA_NOPE = 128
MLA_ROPE = 64
MLA_V = 128
MLA_Q_RANK = 384
MLA_KV_RANK = 256
MLA_HQ = MLA_NOPE + MLA_ROPE
MLA_VX = MLA_V + 16
ROPE_THETA = 10000.0
N_EXPERTS = 16
EC_CAPACITY = 2
EXPERT_FF = 1024
EPS = 1e-6

LANES = 128
VMEM_LIMIT = 52 * 1024 * 1024

BF16 = jnp.bfloat16
F32 = jnp.float32


def _dot(a, b):
    return jnp.dot(a, b, preferred_element_type=F32)


def _dot_nt(a, b):
    return lax.dot_general(a, b, (((1,), (1,)), ((), ())), preferred_element_type=F32)


def _dot_tn(a, b):
    return lax.dot_general(a, b, (((0,), (0,)), ((), ())), preferred_element_type=F32)


def _rms(x, gain):
    ms = jnp.mean(x * x, axis=-1, keepdims=True)
    return x * lax.rsqrt(ms + EPS) * gain


def _sigmoid(x):
    return 0.5 * jnp.tanh(0.5 * x) + 0.5


def _const_spec(shape):
    nd = len(shape)
    return pl.BlockSpec(shape, lambda *_: (0,) * nd, pipeline_mode=pl.Buffered(1))


def _params(sem):
    return pltpu.CompilerParams(dimension_semantics=sem, vmem_limit_bytes=VMEM_LIMIT)


def _gla_in_body(h, wmain_ref, wdec_ref, wa2_ref, ba_ref, wgate_ref,
                 q_ref, k_ref, v_ref, r_ref, la_ref, gate_ref):
    nqk = GLA_HEADS * GLA_DK
    nv = GLA_HEADS * GLA_DV
    dec = _dot(h, wdec_ref[...]).astype(BF16)
    q_ref[...] = (_dot(h, wmain_ref[:, 0:nqk]) * (GLA_DK ** -0.5)).astype(BF16)
    k_ref[...] = _dot(h, wmain_ref[:, nqk:2 * nqk]).astype(BF16)
    z = _dot(dec, wa2_ref[...]) + ba_ref[...]
    log_sig = jnp.minimum(z, 0.0) - jnp.log1p(jnp.exp(-jnp.abs(z)))
    la_ref[...] = log_sig * (1.0 / GLA_TAU)
    v_ref[...] = _dot(h, wmain_ref[:, 2 * nqk:2 * nqk + nv]).astype(BF16)
    r_ref[...] = _dot(h, wmain_ref[:, 2 * nqk + nv:2 * nqk + 2 * nv]).astype(BF16)
    gate_ref[...] = _sigmoid(_dot(h, wgate_ref[...])).astype(BF16)


def _rope128(x, cos, sin_signed):
    lane = lax.broadcasted_iota(jnp.int32, x.shape, 1)
    first_half = (lane % MLA_ROPE) < (MLA_ROPE // 2)
    partner = jnp.where(first_half,
                        pltpu.roll(x, LANES - MLA_ROPE // 2, 1),
                        pltpu.roll(x, MLA_ROPE // 2, 1))
    return x * cos + partner * sin_signed


def _mla_in_body(h, pos_ref, freq_ref, sign_ref, wmla_ref, qn_ref, wuq_ref,
                 kvn_ref, wuk_ref, wvt_ref, qhn_ref, qhr_ref, khn_ref, khr_ref,
                 q_ref, k_ref, vt_ref, *, q_scale):
    lat = _dot(h, wmla_ref[...])
    cq = lat[:, 0:MLA_Q_RANK]
    ckv = lat[:, MLA_Q_RANK:MLA_Q_RANK + MLA_KV_RANK]
    kr2 = lat[:, MLA_Q_RANK + MLA_KV_RANK:MLA_Q_RANK + MLA_KV_RANK + LANES]

    ang = pos_ref[...] * freq_ref[...]
    cos = jnp.cos(ang)
    sin_signed = jnp.sin(ang) * sign_ref[...]
    lane = lax.broadcasted_iota(jnp.int32, ang.shape, 1)
    low = lane < MLA_ROPE

    nn = MLA_HEADS * MLA_NOPE
    qf = _dot(_rms(cq, qn_ref[...]).astype(BF16), wuq_ref[...])
    ckvn = _rms(ckv, kvn_ref[...]).astype(BF16)
    kn = _dot(ckvn, wuk_ref[...])
    vt = _dot_nt(wvt_ref[...], ckvn).astype(BF16)
    pad_row = lax.broadcasted_iota(jnp.int32, (MLA_VX - MLA_V, vt.shape[1]), 0)
    ones_then_zeros = jnp.where(pad_row == 0, 1.0, 0.0).astype(BF16)
    for hd in range(MLA_HEADS):
        vt_ref[hd, 0:MLA_V, :] = vt[hd * MLA_V:(hd + 1) * MLA_V, :]
        vt_ref[hd, MLA_V:MLA_VX, :] = ones_then_zeros

    ss_kr = jnp.sum(kr2 * kr2, axis=-1, keepdims=True)
    kr_rot = _rope128(kr2 * khr_ref[...], cos, sin_signed)[:, 0:MLA_ROPE]

    for pair in range(MLA_HEADS // 2):
        xr = qf[:, nn + pair * LANES: nn + (pair + 1) * LANES]
        sq = xr * xr
        ss_lo = jnp.sum(jnp.where(low, sq, 0.0), axis=-1, keepdims=True)
        ss_hi = jnp.sum(sq, axis=-1, keepdims=True) - ss_lo
        rstd = []
        for j, ss_r in enumerate((ss_lo, ss_hi)):
            hd = 2 * pair + j
            xn = qf[:, hd * MLA_NOPE:(hd + 1) * MLA_NOPE]
            ss = jnp.sum(xn * xn, axis=-1, keepdims=True) + ss_r
            r = lax.rsqrt(ss * (1.0 / MLA_HQ) + EPS)
            rstd.append(r)
            q_ref[hd, :, 0:MLA_NOPE] = (xn * r * (qhn_ref[...] * q_scale)).astype(BF16)
        scaled = xr * jnp.where(low, rstd[0], rstd[1]) * (qhr_ref[...] * q_scale)
        rot = _rope128(scaled, cos, sin_signed).astype(BF16)
        q_ref[2 * pair, :, MLA_NOPE:MLA_HQ] = rot[:, 0:MLA_ROPE]
        q_ref[2 * pair + 1, :, MLA_NOPE:MLA_HQ] = rot[:, MLA_ROPE:LANES]

    for hd in range(MLA_HEADS):
        xn = kn[:, hd * MLA_NOPE:(hd + 1) * MLA_NOPE]
        ss = jnp.sum(xn * xn, axis=-1, keepdims=True) + ss_kr
        r = lax.rsqrt(ss * (1.0 / MLA_HQ) + EPS)
        k_ref[hd, :, 0:MLA_NOPE] = (xn * r * khn_ref[...]).astype(BF16)
        k_ref[hd, :, MLA_NOPE:MLA_HQ] = (kr_rot * r).astype(BF16)


N_GLA_IN, N_GLA_OUT, N_MLA_IN = 5, 6, 13


def _in_proj_kernel(x_ref, g_ref, *refs, q_scale):
    h = _rms(x_ref[...], g_ref[...]).astype(BF16)
    gla_in = refs[:N_GLA_IN]
    mla_in = refs[N_GLA_IN:N_GLA_IN + N_MLA_IN]
    gla_out = refs[N_GLA_IN + N_MLA_IN:N_GLA_IN + N_MLA_IN + N_GLA_OUT]
    mla_out = refs[N_GLA_IN + N_MLA_IN + N_GLA_OUT:]
    _mla_in_body(h, *mla_in, *mla_out, q_scale=q_scale)
    _gla_in_body(h, *gla_in, *gla_out)


def _in_proj(x2, g_mix, gla_consts, pos, mla_consts, B, S, tm, tk, q_scale):
    T = B * S
    nsb = S // tm
    per_kv = tk // tm
    assert tk % tm == 0 and len(gla_consts) == N_GLA_IN and len(mla_consts) == N_MLA_IN - 1
    nqk = GLA_HEADS * GLA_DK
    nv = GLA_HEADS * GLA_DV
    row = lambda n: pl.BlockSpec((tm, n), lambda i: (i, 0))
    head_spec = pl.BlockSpec((None, MLA_HEADS, tm, MLA_HQ), lambda i: (i // nsb, 0, i % nsb, 0))
    in_specs = ([row(D_MODEL), _const_spec(g_mix.shape)] + [_const_spec(c.shape) for c in gla_consts]
                + [row(1)] + [_const_spec(c.shape) for c in mla_consts])
    return pl.pallas_call(
        functools.partial(_in_proj_kernel, q_scale=q_scale),
        grid=(B * nsb,),
        in_specs=in_specs,
        out_specs=[row(nqk), row(nqk), row(nv), row(nv), row(2 * nqk), row(2 * D_MODEL),
                   head_spec, head_spec,
                   pl.BlockSpec((None, MLA_HEADS, None, MLA_VX, tm),
                                lambda i: (i // nsb, 0, (i % nsb) // per_kv, 0, (i % nsb) % per_kv))],
        out_shape=[jax.ShapeDtypeStruct((T, nqk), BF16), jax.ShapeDtypeStruct((T, nqk), BF16),
                   jax.ShapeDtypeStruct((T, nv), BF16), jax.ShapeDtypeStruct((T, nv), BF16),
                   jax.ShapeDtypeStruct((T, 2 * nqk), F32), jax.ShapeDtypeStruct((T, 2 * D_MODEL), BF16),
                   jax.ShapeDtypeStruct((B, MLA_HEADS, S, MLA_HQ), BF16),
                   jax.ShapeDtypeStruct((B, MLA_HEADS, S, MLA_HQ), BF16),
                   jax.ShapeDtypeStruct((B, MLA_HEADS, S // tk, MLA_VX, tk), BF16)],
        compiler_params=_params(("parallel",)),
        name="in_proj",
    )(x2, g_mix, *gla_consts, pos, *mla_consts)


def _gla_block(q_ref, k_ref, v_ref, la_ref, st_ref, emit, *, reverse, n_chunks):
    C = GLA_CHUNK
    row = lax.broadcasted_iota(jnp.int32, (C, C), 0)
    col = lax.broadcasted_iota(jnp.int32, (C, C), 1)
    if reverse:
        cum_mat = (col >= row).astype(BF16)
        att_mask = col > row
        last = 0
    else:
        cum_mat = (col <= row).astype(BF16)
        att_mask = col <= row
        last = C - 1
    order = range(n_chunks - 1, -1, -1) if reverse else range(n_chunks)
    for c in order:
        rows = slice(c * C, (c + 1) * C)
        la = la_ref[rows, :]
        la_hi = la.astype(BF16)
        la_lo = (la - la_hi.astype(F32)).astype(BF16)
        b_all = _dot(cum_mat, la_hi) + _dot(cum_mat, la_lo)
        for hd in range(GLA_HEADS):
            kcols = slice(hd * GLA_DK, (hd + 1) * GLA_DK)
            vcols = slice(hd * GLA_DV, (hd + 1) * GLA_DV)
            b = b_all[:, kcols]
            b_last = b[last:last + 1, :]
            q = q_ref[rows, kcols].astype(F32)
            k = k_ref[rows, kcols].astype(F32)
            v = v_ref[rows, vcols]
            q_t = (q * jnp.exp(b)).astype(BF16)
            k_t = (k * jnp.exp(-b)).astype(BF16)
            k_dec = (k * jnp.exp(b_last - b)).astype(BF16)
            att = jnp.where(att_mask, _dot_nt(q_t, k_t), 0.0).astype(BF16)
            st = st_ref[hd]
            o = _dot(att, v) + _dot_nt(q_t, st.astype(BF16))
            st_ref[hd] = st * jnp.exp(b_last) + _dot_tn(v, k_dec)
            emit(rows, vcols, o)


def _gla_kernel(qf_ref, kf_ref, vf_ref, laf_ref, qb_ref, kb_ref, vb_ref, lab_ref, of_ref, ob_ref,
                stf_ref, stb_ref, *, n_chunks):
    @pl.when(pl.program_id(1) == 0)
    def _():
        stf_ref[...] = jnp.zeros_like(stf_ref)
        stb_ref[...] = jnp.zeros_like(stb_ref)

    def emit_f(rows, vcols, o):
        of_ref[rows, vcols] = o.astype(BF16)

    def emit_b(rows, vcols, o):
        ob_ref[rows, vcols] = o.astype(BF16)

    _gla_block(qf_ref, kf_ref, vf_ref, laf_ref, stf_ref, emit_f, reverse=False, n_chunks=n_chunks)
    _gla_block(qb_ref, kb_ref, vb_ref, lab_ref, stb_ref, emit_b, reverse=True, n_chunks=n_chunks)


def _gla(gq, gk, gv, la, B, S, tb):
    nb = S // tb
    nqk = GLA_HEADS * GLA_DK
    nv = GLA_HEADS * GLA_DV
    state = pltpu.VMEM((GLA_HEADS, GLA_DV, GLA_DK), F32)
    fwd = lambda n, c=0: pl.BlockSpec((tb, n), lambda b, i: (b * nb + i, c))
    rev = lambda n, c=0: pl.BlockSpec((tb, n), lambda b, i: (b * nb + nb - 1 - i, c))
    return pl.pallas_call(
        functools.partial(_gla_kernel, n_chunks=tb // GLA_CHUNK),
        grid=(B, nb),
        in_specs=[fwd(nqk), fwd(nqk), fwd(nv), fwd(nqk, 0), rev(nqk), rev(nqk), rev(nv), rev(nqk, 1)],
        out_specs=[fwd(nv), rev(nv)],
        out_shape=[jax.ShapeDtypeStruct((B * S, nv), BF16)] * 2,
        scratch_shapes=[state, state],
        compiler_params=_params(("parallel", "arbitrary")),
        name="gla_scan",
    )(gq, gk, gv, la, gq, gk, gv, la)


ATTN_REF_MAX = 48.0


def _attn_kernel(qmax_ref, q_ref, k_ref, vt_ref, o_ref, acc_ref, s_ref, p_ref, mx_ref, al_ref, ref_smem,
                 *, tk, n_kv):
    q = q_ref[...]
    tq = q.shape[0]
    n_pairs = n_kv // 2

    @pl.when(pl.program_id(2) == 0)
    def _():
        kk = k_ref[...].astype(F32)
        k_sq = _dot_nt(jnp.ones((8, MLA_HQ), BF16), (kk * kk).astype(BF16))
        k_max = jnp.sqrt(jnp.max(k_sq, axis=1, keepdims=True))
        ref_smem[0] = qmax_ref[0, 0] * k_max[0, 0]

    ref = ref_smem[0]

    def k_tile(pair, t):
        off = (2 * pair + t) * tk
        return k_ref[pl.ds(off if isinstance(off, int) else pl.multiple_of(off, tk), tk), :]

    def finish():
        inv_l = 1.0 / acc_ref[MLA_V:MLA_V + 1, :]
        o_ref[...] = (acc_ref[0:MLA_V, :] * inv_l).T.astype(BF16)

    @pl.when(ref <= ATTN_REF_MAX)
    def _bounded():
        def scores_exp(pair, sw):
            for t in range(2):
                p_ref[2 * sw + t] = jnp.exp2(_dot_nt(k_tile(pair, t), q) - ref).astype(BF16)

        def accumulate(pair, sw):
            p2 = p_ref[2 * sw:2 * sw + 2].reshape(2 * tk, tq)
            acc_ref[...] += _dot(vt_ref[pair], p2)

        acc_ref[...] = jnp.zeros_like(acc_ref)
        scores_exp(0, 0)
        for pair in range(1, n_pairs):
            scores_exp(pair, pair % 2)
            accumulate(pair - 1, (pair - 1) % 2)
        accumulate(n_pairs - 1, (n_pairs - 1) % 2)
        finish()

    @pl.when(ref > ATTN_REF_MAX)
    def _online():
        def scores(pair, sw):
            for t in range(2):
                s = _dot_nt(k_tile(pair, t), q)
                s_ref[2 * sw + t] = s
                mx_ref[2 * sw + t] = jnp.max(s, axis=0, keepdims=True)

        def softmax(sw, m):
            for t in range(2):
                slot = 2 * sw + t
                m_new = jnp.maximum(m, mx_ref[slot])
                al_ref[slot] = jnp.exp2(m - m_new)
                p_ref[slot] = jnp.exp2(s_ref[slot] - m_new).astype(BF16)
                m = m_new
            return m

        def accumulate(pair, sw):
            for t in range(2):
                slot = 2 * sw + t
                acc_ref[...] = (al_ref[slot] * acc_ref[...]
                                + _dot(vt_ref[pair, :, t * tk:(t + 1) * tk], p_ref[slot]))

        acc_ref[...] = jnp.zeros_like(acc_ref)
        scores(0, 0)
        scores(1, 1)
        m = softmax(0, jnp.full((1, tq), -1e30, F32))

        def body(it, m):
            pair = 1 + 2 * it
            scores(pair + 1, 0)
            m = softmax(1, m)
            accumulate(pair - 1, 0)
            scores(pair + 2, 1)
            m = softmax(0, m)
            accumulate(pair, 1)
            return m

        m = lax.fori_loop(0, (n_pairs - 2) // 2, body, m)
        softmax(1, m)
        accumulate(n_pairs - 2, 0)
        accumulate(n_pairs - 1, 1)
        finish()


def _attn(qmax, q, k, vt, B, S, tq, tk):
    assert (S // tk) % 4 == 0, "kv tiles are consumed in pairs, two pairs per loop pass"
    return pl.pallas_call(
        functools.partial(_attn_kernel, tk=tk, n_kv=S // tk),
        grid=(B, MLA_HEADS, S // tq),
        in_specs=[pl.BlockSpec(memory_space=pltpu.SMEM),
                  pl.BlockSpec((None, None, tq, MLA_HQ), lambda b, h, i: (b, h, i, 0)),
                  pl.BlockSpec((None, None, S, MLA_HQ), lambda b, h, i: (b, h, 0, 0)),
                  pl.BlockSpec((None, None, S // (2 * tk), MLA_VX, 2 * tk), lambda b, h, i: (b, h, 0, 0, 0))],
        out_specs=pl.BlockSpec((None, tq, MLA_V), lambda b, h, i: (b, i, h)),
        out_shape=jax.ShapeDtypeStruct((B, S, MLA_HEADS * MLA_V), BF16),
        scratch_shapes=[pltpu.VMEM((MLA_VX, tq), F32), pltpu.VMEM((4, tk, tq), F32),
                        pltpu.VMEM((4, tk, tq), BF16), pltpu.VMEM((4, 1, tq), F32),
                        pltpu.VMEM((4, 1, tq), F32), pltpu.SMEM((1,), F32)],
        compiler_params=_params(("parallel", "parallel", "arbitrary")),
        name="mla_attn",
    )(qmax, q, k, vt)


def _post_mix_kernel(x_ref, of_ref, og_ref, r_ref, gn_ref, ob_ref, gate_ref, wga_ref, wmo_ref, wo_ref,
                     gf_ref, wrt_ref, x1_ref, h2_ref, aff_ref):
    heads = []
    for hd in range(GLA_HEADS):
        cols = slice(hd * GLA_DV, (hd + 1) * GLA_DV)
        tot = of_ref[:, cols].astype(F32) + og_ref[:, cols].astype(F32)
        gate = r_ref[:, cols].astype(F32)
        heads.append((_rms(tot, gn_ref[...]) * (gate * _sigmoid(gate))).astype(BF16))
    ya = _dot(jnp.concatenate(heads, axis=1), wga_ref[...])
    yb = _dot(ob_ref[...], wmo_ref[...])
    ga = gate_ref[:, 0:D_MODEL].astype(F32)
    gb = gate_ref[:, D_MODEL:2 * D_MODEL].astype(F32)
    merged = (ga * ya + gb * yb).astype(BF16)
    x1 = x_ref[...] + _dot(merged, wo_ref[...])
    x1_ref[...] = x1
    h2 = _rms(x1, gf_ref[...]).astype(BF16)
    h2_ref[...] = h2
    logits = _dot_nt(wrt_ref[...], h2)
    e = jnp.exp(logits - jnp.max(logits, axis=0, keepdims=True))
    aff_ref[...] = e / jnp.sum(e, axis=0, keepdims=True)


def _post_mix(x2, o_fwd, o_bwd, gr, gnorm, ob, gates, w_ga, w_mo, w_o, g_ffn, w_rt, B, S, tm):
    nsb = S // tm
    row = lambda n: pl.BlockSpec((tm, n), lambda i: (i, 0))
    return pl.pallas_call(
        _post_mix_kernel,
        grid=(B * nsb,),
        in_specs=[row(D_MODEL), row(D_MODEL), row(D_MODEL), row(D_MODEL), _const_spec(gnorm.shape),
                  row(D_MODEL), row(2 * D_MODEL), _const_spec(w_ga.shape), _const_spec(w_mo.shape), _const_spec(w_o.shape),
                  _const_spec(g_ffn.shape), _const_spec(w_rt.shape)],
        out_specs=[row(D_MODEL), row(D_MODEL),
                   pl.BlockSpec((None, N_EXPERTS, tm), lambda i: (i // nsb, 0, i % nsb))],
        out_shape=[jax.ShapeDtypeStruct((B * S, D_MODEL), F32),
                   jax.ShapeDtypeStruct((B * S, D_MODEL), BF16),
                   jax.ShapeDtypeStruct((B, N_EXPERTS, S), F32)],
        compiler_params=_params(("parallel",)),
        name="post_mix",
    )(x2, o_fwd, o_bwd, gr, gnorm, ob, gates, w_ga, w_mo, w_o, g_ffn, w_rt)


def _select_kernel(aff_ref, pos_ref, pre_ref, *, cap, pblk):
    aff = aff_ref[...]
    S = aff.shape[1]
    bits = pltpu.bitcast(aff, jnp.int32)

    def search(i, thr):
        cand = thr | jnp.left_shift(jnp.int32(1), 30 - i)
        cnt = jnp.sum(jnp.where(bits >= cand, 1.0, 0.0), axis=1, keepdims=True)
        return jnp.where(cnt >= cap, cand, thr)

    thr = lax.fori_loop(0, 31, search, jnp.zeros((aff.shape[0], 1), jnp.int32))
    gt = bits > thr
    eq = bits == thr
    need = cap - jnp.sum(jnp.where(gt, 1.0, 0.0), axis=1, keepdims=True)

    r = lax.broadcasted_iota(jnp.int32, (pblk, pblk), 0)
    c = lax.broadcasted_iota(jnp.int32, (pblk, pblk), 1)
    before = (r < c).astype(BF16)

    def prefix_blocks(mask_of_block):
        carry = jnp.zeros((aff.shape[0], 1), F32)
        out = []
        for j in range(S // pblk):
            mk = mask_of_block(j)
            out.append(_dot(mk.astype(BF16), before) + carry)
            carry = carry + jnp.sum(mk, axis=1, keepdims=True)
        return out

    blk = lambda a, j: a[:, j * pblk:(j + 1) * pblk]
    tie_pre = prefix_blocks(lambda j: jnp.where(blk(eq, j), 1.0, 0.0))
    sel = [jnp.where(blk(gt, j) | (blk(eq, j) & (tie_pre[j] < need)), 1.0, 0.0)
           for j in range(S // pblk)]
    sel_pre = prefix_blocks(lambda j: sel[j])
    for j in range(S // pblk):
        pre_i = sel_pre[j].astype(jnp.int32)
        pre_ref[:, j * pblk:(j + 1) * pblk] = pre_i
        pos_ref[:, j * pblk:(j + 1) * pblk] = jnp.where(sel[j] > 0.5, pre_i, -1)


def _select(aff_t, cap, pblk):
    B, E, S = aff_t.shape
    spec = pl.BlockSpec((None, E, S), lambda b: (b, 0, 0))
    return pl.pallas_call(
        functools.partial(_select_kernel, cap=cap, pblk=pblk),
        grid=(B,),
        in_specs=[spec],
        out_specs=[spec, spec],
        out_shape=[jax.ShapeDtypeStruct((B, E, S), jnp.int32)] * 2,
        compiler_params=_params(("parallel",)),
        name="ec_select",
    )(aff_t)


def _ec_ffn_kernel(bs_ref, pos_ref, val_ref, h2_ref, wg_ref, wu_ref, wd_ref, x1_hbm,
                   out_ref, sem, *, n_parts, n_sub, tp, grp):
    b, part, e = pl.program_id(0), pl.program_id(1), pl.program_id(2)

    @pl.when(e == 0)
    def _():
        row0 = pl.multiple_of((b * n_parts + part) * tp, tp)
        cp = pltpu.make_async_copy(x1_hbm.at[pl.ds(row0, tp), :], out_ref, sem)
        cp.start()
        cp.wait()

    ts = tp // n_sub
    base = (b * N_EXPERTS + e) * (n_parts * n_sub + 1) + part * n_sub
    slot_iota = lax.broadcasted_iota(jnp.int32, (grp, ts), 0)

    for k in range(n_sub):
        s_lo = bs_ref[base + k]
        s_hi = bs_ref[base + k + 1]
        tok = slice(k * ts, (k + 1) * ts)

        def group(g, _, s_lo=s_lo, tok=tok):
            hit = pos_ref[:, tok] == (slot_iota + (s_lo + g * grp))
            xe = _dot(jnp.where(hit, 1.0, 0.0).astype(BF16), h2_ref[tok, :]).astype(BF16)
            a = _dot(xe, wg_ref[...])
            u = _dot(xe, wu_ref[...])
            mid = (a * _sigmoid(a) * u).astype(BF16)
            ye = _dot(mid, wd_ref[...]).astype(BF16)
            w = jnp.where(hit, val_ref[:, tok], 0.0).astype(BF16)
            out_ref[tok, :] += _dot_tn(w, ye)
            return 0

        lax.fori_loop(0, (s_hi - s_lo + grp - 1) // grp, group, 0)


def _ec_ffn(bs, pos4, val4, h2, wg, wu, wd, x1, B, S, n_parts, n_sub, grp):
    tp = S // n_parts
    row_spec = pl.BlockSpec((None, None, 1, tp), lambda b, p, e, bs_: (b, e, 0, p))
    w_spec = lambda shp: pl.BlockSpec((None,) + shp, lambda b, p, e, bs_: (e, 0, 0))
    part_spec = pl.BlockSpec((tp, D_MODEL), lambda b, p, e, bs_: (b * n_parts + p, 0))
    grid_spec = pltpu.PrefetchScalarGridSpec(
        num_scalar_prefetch=1,
        grid=(B, n_parts, N_EXPERTS),
        in_specs=[row_spec, row_spec, part_spec,
                  w_spec((D_MODEL, EXPERT_FF)), w_spec((D_MODEL, EXPERT_FF)),
                  w_spec((EXPERT_FF, D_MODEL)), pl.BlockSpec(memory_space=pl.ANY)],
        out_specs=part_spec,
        scratch_shapes=[pltpu.SemaphoreType.DMA(())],
    )
    return pl.pallas_call(
        functools.partial(_ec_ffn_kernel, n_parts=n_parts, n_sub=n_sub, tp=tp, grp=grp),
        grid_spec=grid_spec,
        out_shape=jax.ShapeDtypeStruct((B * S, D_MODEL), F32),
        compiler_params=_params(("parallel", "parallel", "arbitrary")),
        name="ec_ffn",
    )(bs, pos4, val4, h2, wg, wu, wd, x1)


def _tiles(S):
    return dict(tm=min(256, S), tb=min(256, S), tq=min(1024, S), tk=min(512, S // 4),
                tmix=min(512, S), pblk=min(256, S),
                n_parts=max(1, min(4, S // 256)), n_sub=2, grp=160)


def kernel(x, positions, g_mix, w_in, gla_wa2_f, gla_ba_f, gla_wa2_b, gla_ba_b, gla_onorm,
           w_gla_out, mla_q_norm, w_uq, mla_kv_norm, w_ukv, q_head_norm, k_head_norm, w_mla_out,
           w_o, g_ffn, w_router, w_gate_e, w_up_e, w_down_e):
    B, S, D = x.shape
    assert D == D_MODEL and S % GLA_CHUNK == 0
    t = _tiles(S)
    T = B * S
    x2 = x.reshape(T, D)
    row = lambda v: v.reshape(1, -1).astype(F32)

    nqk, nv = GLA_HEADS * GLA_DK, GLA_HEADS * GLA_DV
    o_dec = 2 * nqk + 2 * nv
    o_mla = o_dec + 2 * GLA_GATE_RANK
    o_gate = o_mla + MLA_Q_RANK + MLA_KV_RANK + MLA_ROPE
    w_main = w_in[:, :o_dec].astype(BF16)
    w_dec = jnp.pad(w_in[:, o_dec:o_mla], ((0, 0), (0, LANES - 2 * GLA_GATE_RANK))).astype(BF16)
    w_mla = jnp.pad(w_in[:, o_mla:o_gate], ((0, 0), (0, MLA_ROPE))).astype(BF16)
    w_gates = w_in[:, o_gate:].astype(BF16)
    wa2 = jnp.zeros((LANES, 2 * nqk), F32)
    wa2 = wa2.at[0:GLA_GATE_RANK, 0:nqk].set(gla_wa2_f)
    wa2 = wa2.at[GLA_GATE_RANK:2 * GLA_GATE_RANK, nqk:].set(gla_wa2_b).astype(BF16)
    ba = jnp.concatenate([gla_ba_f, gla_ba_b]).reshape(1, -1)


    w_uq3 = w_uq.reshape(MLA_Q_RANK, MLA_HEADS, MLA_HQ)
    w_uq_r = jnp.concatenate([w_uq3[:, :, :MLA_NOPE].reshape(MLA_Q_RANK, -1),
                              w_uq3[:, :, MLA_NOPE:].reshape(MLA_Q_RANK, -1)], axis=1).astype(BF16)
    w_ukv3 = w_ukv.reshape(MLA_KV_RANK, MLA_HEADS, MLA_NOPE + MLA_V)
    w_uk = w_ukv3[:, :, :MLA_NOPE].reshape(MLA_KV_RANK, -1).astype(BF16)
    w_vt = w_ukv3[:, :, MLA_NOPE:].reshape(MLA_KV_RANK, -1).T.astype(BF16)
    half = MLA_ROPE // 2
    freqs = ROPE_THETA ** (-jnp.arange(half, dtype=F32) / half)
    freq128 = jnp.tile(freqs, LANES // half).reshape(1, LANES)
    sign128 = jnp.tile(jnp.concatenate([-jnp.ones(half, F32), jnp.ones(half, F32)]), 2).reshape(1, LANES)
    qhr = jnp.tile(q_head_norm[MLA_NOPE:], 2).reshape(1, LANES)
    khr = jnp.pad(k_head_norm[MLA_NOPE:], (0, MLA_ROPE)).reshape(1, LANES)
    q_scale = math.log2(math.e) / math.sqrt(MLA_HQ)
    mla_consts = (freq128, sign128, w_mla, row(mla_q_norm), w_uq_r, row(mla_kv_norm), w_uk, w_vt,
                  row(q_head_norm[:MLA_NOPE]), qhr, row(k_head_norm[:MLA_NOPE]), khr)
    gq, gk, gv, gr, la, gates, q_b, k_b, v_t = _in_proj(
        x2, row(g_mix), (w_main, w_dec, wa2, ba, w_gates), positions.reshape(T, 1).astype(F32), mla_consts,
        B, S, min(t["tm"], t["tk"]), 2 * t["tk"], q_scale)

    o_fwd, o_bwd = _gla(gq, gk, gv, la, B, S, t["tb"])
    q_bound = (jnp.max(jnp.abs(q_head_norm)) * (math.sqrt(MLA_HQ) * q_scale)).reshape(1, 1).astype(F32)
    o_b = _attn(q_bound, q_b, k_b, v_t, B, S, t["tq"], t["tk"]).reshape(T, MLA_HEADS * MLA_V)

    x1, h2, aff_t = _post_mix(x2, o_fwd, o_bwd, gr, row(gla_onorm), o_b, gates, w_gla_out.astype(BF16),
                              w_mla_out.astype(BF16), w_o.astype(BF16), row(g_ffn),
                              w_router.T.astype(BF16), B, S, t["tmix"])

    cap = EC_CAPACITY * S // N_EXPERTS
    pos, pre = _select(aff_t, cap, t["pblk"])
    sub_tokens = S // (t["n_parts"] * t["n_sub"])
    bs = jnp.concatenate([pre[:, :, ::sub_tokens], jnp.full((B, N_EXPERTS, 1), cap, jnp.int32)], axis=2)
    out = _ec_ffn(bs.reshape(-1), pos.reshape(B, N_EXPERTS, 1, S), aff_t.reshape(B, N_EXPERTS, 1, S),
                  h2, w_gate_e.astype(BF16), w_up_e.astype(BF16), w_down_e.astype(BF16), x1,
                  B, S, t["n_parts"], t["n_sub"], t["grp"])
    return out.reshape(B, S, D)
```

```python
import functools
import math

import jax
import jax.numpy as jnp
from jax import lax
from jax.experimental import pallas as pl
from jax.experimental.pallas import tpu as pltpu

D_MODEL = 1024
GLA_HEADS = 4
GLA_DK = 128
GLA_DV = 256
GLA_GATE_RANK = 16
GLA_TAU = 16.0
GLA_CHUNK = 64
MLA_HEADS = 8
MLA_NOPE = 128
MLA_ROPE = 64
MLA_V = 128
MLA_Q_RANK = 384
MLA_KV_RANK = 256
MLA_HQ = MLA_NOPE + MLA_ROPE
MLA_VX = MLA_V + 16
ROPE_THETA = 10000.0
N_EXPERTS = 16
EC_CAPACITY = 2
EXPERT_FF = 1024
EPS = 1e-6

LANES = 128
VMEM_LIMIT = 52 * 1024 * 1024

BF16 = jnp.bfloat16
F32 = jnp.float32


def _dot(a, b):
    return jnp.dot(a, b, preferred_element_type=F32)


def _dot_nt(a, b):
    return lax.dot_general(a, b, (((1,), (1,)), ((), ())), preferred_element_type=F32)


def _dot_tn(a, b):
    return lax.dot_general(a, b, (((0,), (0,)), ((), ())), preferred_element_type=F32)


def _rms(x, gain):
    ms = jnp.mean(x * x, axis=-1, keepdims=True)
    return x * lax.rsqrt(ms + EPS) * gain


def _sigmoid(x):
    return 0.5 * jnp.tanh(0.5 * x) + 0.5


def _const_spec(shape):
    nd = len(shape)
    return pl.BlockSpec(shape, lambda *_: (0,) * nd, pipeline_mode=pl.Buffered(1))


def _params(sem):
    return pltpu.CompilerParams(dimension_semantics=sem, vmem_limit_bytes=VMEM_LIMIT)


def _gla_in_body(h, wmain_ref, wdec_ref, wa2_ref, ba_ref, wgate_ref,
                 q_ref, k_ref, v_ref, r_ref, la_ref, gate_ref):
    nqk = GLA_HEADS * GLA_DK
    nv = GLA_HEADS * GLA_DV
    dec = _dot(h, wdec_ref[...]).astype(BF16)
    q_ref[...] = (_dot(h, wmain_ref[:, 0:nqk]) * (GLA_DK ** -0.5)).astype(BF16)
    k_ref[...] = _dot(h, wmain_ref[:, nqk:2 * nqk]).astype(BF16)
    z = _dot(dec, wa2_ref[...]) + ba_ref[...]
    log_sig = jnp.minimum(z, 0.0) - jnp.log1p(jnp.exp(-jnp.abs(z)))
    la_ref[...] = log_sig * (1.0 / GLA_TAU)
    v_ref[...] = _dot(h, wmain_ref[:, 2 * nqk:2 * nqk + nv]).astype(BF16)
    r_ref[...] = _dot(h, wmain_ref[:, 2 * nqk + nv:2 * nqk + 2 * nv]).astype(BF16)
    gate_ref[...] = _sigmoid(_dot(h, wgate_ref[...])).astype(BF16)


def _rope128(x, cos, sin_signed):
    lane = lax.broadcasted_iota(jnp.int32, x.shape, 1)
    first_half = (lane % MLA_ROPE) < (MLA_ROPE // 2)
    partner = jnp.where(first_half,
                        pltpu.roll(x, LANES - MLA_ROPE // 2, 1),
                        pltpu.roll(x, MLA_ROPE // 2, 1))
    return x * cos + partner * sin_signed


def _mla_in_body(h, pos_ref, freq_ref, sign_ref, wmla_ref, qn_ref, wuq_ref,
                 kvn_ref, wuk_ref, wvt_ref, qhn_ref, qhr_ref, khn_ref, khr_ref,
                 q_ref, k_ref, vt_ref, *, q_scale):
    lat = _dot(h, wmla_ref[...])
    cq = lat[:, 0:MLA_Q_RANK]
    ckv = lat[:, MLA_Q_RANK:MLA_Q_RANK + MLA_KV_RANK]
    kr2 = lat[:, MLA_Q_RANK + MLA_KV_RANK:MLA_Q_RANK + MLA_KV_RANK + LANES]

    ang = pos_ref[...] * freq_ref[...]
    cos = jnp.cos(ang)
    sin_signed = jnp.sin(ang) * sign_ref[...]
    lane = lax.broadcasted_iota(jnp.int32, ang.shape, 1)
    low = lane < MLA_ROPE

    nn = MLA_HEADS * MLA_NOPE
    qf = _dot(_rms(cq, qn_ref[...]).astype(BF16), wuq_ref[...])
    ckvn = _rms(ckv, kvn_ref[...]).astype(BF16)
    kn = _dot(ckvn, wuk_ref[...])
    vt = _dot_nt(wvt_ref[...], ckvn).astype(BF16)
    pad_row = lax.broadcasted_iota(jnp.int32, (MLA_VX - MLA_V, vt.shape[1]), 0)
    ones_then_zeros = jnp.where(pad_row == 0, 1.0, 0.0).astype(BF16)
    for hd in range(MLA_HEADS):
        vt_ref[hd, 0:MLA_V, :] = vt[hd * MLA_V:(hd + 1) * MLA_V, :]
        vt_ref[hd, MLA_V:MLA_VX, :] = ones_then_zeros

    ss_kr = jnp.sum(kr2 * kr2, axis=-1, keepdims=True)
    kr_rot = _rope128(kr2 * khr_ref[...], cos, sin_signed)[:, 0:MLA_ROPE]

    for pair in range(MLA_HEADS // 2):
        xr = qf[:, nn + pair * LANES: nn + (pair + 1) * LANES]
        sq = xr * xr
        ss_lo = jnp.sum(jnp.where(low, sq, 0.0), axis=-1, keepdims=True)
        ss_hi = jnp.sum(sq, axis=-1, keepdims=True) - ss_lo
        rstd = []
        for j, ss_r in enumerate((ss_lo, ss_hi)):
            hd = 2 * pair + j
            xn = qf[:, hd * MLA_NOPE:(hd + 1) * MLA_NOPE]
            ss = jnp.sum(xn * xn, axis=-1, keepdims=True) + ss_r
            r = lax.rsqrt(ss * (1.0 / MLA_HQ) + EPS)
            rstd.append(r)
            q_ref[hd, :, 0:MLA_NOPE] = (xn * r * (qhn_ref[...] * q_scale)).astype(BF16)
        scaled = xr * jnp.where(low, rstd[0], rstd[1]) * (qhr_ref[...] * q_scale)
        rot = _rope128(scaled, cos, sin_signed).astype(BF16)
        q_ref[2 * pair, :, MLA_NOPE:MLA_HQ] = rot[:, 0:MLA_ROPE]
        q_ref[2 * pair + 1, :, MLA_NOPE:MLA_HQ] = rot[:, MLA_ROPE:LANES]

    for hd in range(MLA_HEADS):
        xn = kn[:, hd * MLA_NOPE:(hd + 1) * MLA_NOPE]
        ss = jnp.sum(xn * xn, axis=-1, keepdims=True) + ss_kr
        r = lax.rsqrt(ss * (1.0 / MLA_HQ) + EPS)
        k_ref[hd, :, 0:MLA_NOPE] = (xn * r * khn_ref[...]).astype(BF16)
        k_ref[hd, :, MLA_NOPE:MLA_HQ] = (kr_rot * r).astype(BF16)


N_GLA_IN, N_GLA_OUT, N_MLA_IN = 5, 6, 13


def _in_proj_kernel(x_ref, g_ref, *refs, q_scale):
    h = _rms(x_ref[...], g_ref[...]).astype(BF16)
    gla_in = refs[:N_GLA_IN]
    mla_in = refs[N_GLA_IN:N_GLA_IN + N_MLA_IN]
    gla_out = refs[N_GLA_IN + N_MLA_IN:N_GLA_IN + N_MLA_IN + N_GLA_OUT]
    mla_out = refs[N_GLA_IN + N_MLA_IN + N_GLA_OUT:]
    _mla_in_body(h, *mla_in, *mla_out, q_scale=q_scale)
    _gla_in_body(h, *gla_in, *gla_out)


def _in_proj(x2, g_mix, gla_consts, pos, mla_consts, B, S, tm, tk, q_scale):
    T = B * S
    nsb = S // tm
    per_kv = tk // tm
    assert tk % tm == 0 and len(gla_consts) == N_GLA_IN and len(mla_consts) == N_MLA_IN - 1
    nqk = GLA_HEADS * GLA_DK
    nv = GLA_HEADS * GLA_DV
    row = lambda n: pl.BlockSpec((tm, n), lambda i: (i, 0))
    head_spec = pl.BlockSpec((None, MLA_HEADS, tm, MLA_HQ), lambda i: (i // nsb, 0, i % nsb, 0))
    in_specs = ([row(D_MODEL), _const_spec(g_mix.shape)] + [_const_spec(c.shape) for c in gla_consts]
                + [row(1)] + [_const_spec(c.shape) for c in mla_consts])
    return pl.pallas_call(
        functools.partial(_in_proj_kernel, q_scale=q_scale),
        grid=(B * nsb,),
        in_specs=in_specs,
        out_specs=[row(nqk), row(nqk), row(nv), row(nv), row(2 * nqk), row(2 * D_MODEL),
                   head_spec, head_spec,
                   pl.BlockSpec((None, MLA_HEADS, None, MLA_VX, tm),
                                lambda i: (i // nsb, 0, (i % nsb) // per_kv, 0, (i % nsb) % per_kv))],
        out_shape=[jax.ShapeDtypeStruct((T, nqk), BF16), jax.ShapeDtypeStruct((T, nqk), BF16),
                   jax.ShapeDtypeStruct((T, nv), BF16), jax.ShapeDtypeStruct((T, nv), BF16),
                   jax.ShapeDtypeStruct((T, 2 * nqk), F32), jax.ShapeDtypeStruct((T, 2 * D_MODEL), BF16),
                   jax.ShapeDtypeStruct((B, MLA_HEADS, S, MLA_HQ), BF16),
                   jax.ShapeDtypeStruct((B, MLA_HEADS, S, MLA_HQ), BF16),
                   jax.ShapeDtypeStruct((B, MLA_HEADS, S // tk, MLA_VX, tk), BF16)],
        compiler_params=_params(("parallel",)),
        name="in_proj",
    )(x2, g_mix, *gla_consts, pos, *mla_consts)


def _gla_block(q_ref, k_ref, v_ref, la_ref, st_ref, emit, *, reverse, n_chunks):
    C = GLA_CHUNK
    row = lax.broadcasted_iota(jnp.int32, (C, C), 0)
    col = lax.broadcasted_iota(jnp.int32, (C, C), 1)
    if reverse:
        cum_mat = (col >= row).astype(BF16)
        att_mask = col > row
        last = 0
    else:
        cum_mat = (col <= row).astype(BF16)
        att_mask = col <= row
        last = C - 1
    order = range(n_chunks - 1, -1, -1) if reverse else range(n_chunks)
    for c in order:
        rows = slice(c * C, (c + 1) * C)
        la = la_ref[rows, :]
        la_hi = la.astype(BF16)
        la_lo = (la - la_hi.astype(F32)).astype(BF16)
        b_all = _dot(cum_mat, la_hi) + _dot(cum_mat, la_lo)
        for hd in range(GLA_HEADS):
            kcols = slice(hd * GLA_DK, (hd + 1) * GLA_DK)
            vcols = slice(hd * GLA_DV, (hd + 1) * GLA_DV)
            b = b_all[:, kcols]
            b_last = b[last:last + 1, :]
            q = q_ref[rows, kcols].astype(F32)
            k = k_ref[rows, kcols].astype(F32)
            v = v_ref[rows, vcols]
            q_t = (q * jnp.exp(b)).astype(BF16)
            k_t = (k * jnp.exp(-b)).astype(BF16)
            k_dec = (k * jnp.exp(b_last - b)).astype(BF16)
            att = jnp.where(att_mask, _dot_nt(q_t, k_t), 0.0).astype(BF16)
            st = st_ref[hd]
            o = _dot(att, v) + _dot_nt(q_t, st.astype(BF16))
            st_ref[hd] = st * jnp.exp(b_last) + _dot_tn(v, k_dec)
            emit(rows, vcols, o)


def _gla_kernel(qf_ref, kf_ref, vf_ref, laf_ref, qb_ref, kb_ref, vb_ref, lab_ref, of_ref, ob_ref,
                stf_ref, stb_ref, *, n_chunks):
    @pl.when(pl.program_id(1) == 0)
    def _():
        stf_ref[...] = jnp.zeros_like(stf_ref)
        stb_ref[...] = jnp.zeros_like(stb_ref)

    def emit_f(rows, vcols, o):
        of_ref[rows, vcols] = o.astype(BF16)

    def emit_b(rows, vcols, o):
        ob_ref[rows, vcols] = o.astype(BF16)

    _gla_block(qf_ref, kf_ref, vf_ref, laf_ref, stf_ref, emit_f, reverse=False, n_chunks=n_chunks)
    _gla_block(qb_ref, kb_ref, vb_ref, lab_ref, stb_ref, emit_b, reverse=True, n_chunks=n_chunks)


def _gla(gq, gk, gv, la, B, S, tb):
    nb = S // tb
    nqk = GLA_HEADS * GLA_DK
    nv = GLA_HEADS * GLA_DV
    state = pltpu.VMEM((GLA_HEADS, GLA_DV, GLA_DK), F32)
    fwd = lambda n, c=0: pl.BlockSpec((tb, n), lambda b, i: (b * nb + i, c))
    rev = lambda n, c=0: pl.BlockSpec((tb, n), lambda b, i: (b * nb + nb - 1 - i, c))
    return pl.pallas_call(
        functools.partial(_gla_kernel, n_chunks=tb // GLA_CHUNK),
        grid=(B, nb),
        in_specs=[fwd(nqk), fwd(nqk), fwd(nv), fwd(nqk, 0), rev(nqk), rev(nqk), rev(nv), rev(nqk, 1)],
        out_specs=[fwd(nv), rev(nv)],
        out_shape=[jax.ShapeDtypeStruct((B * S, nv), BF16)] * 2,
        scratch_shapes=[state, state],
        compiler_params=_params(("parallel", "arbitrary")),
        name="gla_scan",
    )(gq, gk, gv, la, gq, gk, gv, la)


ATTN_REF_MAX = 48.0


def _attn_kernel(qmax_ref, q_ref, k_ref, vt_ref, o_ref, acc_ref, s_ref, p_ref, mx_ref, al_ref, ref_smem,
                 *, tk, n_kv):
    q = q_ref[...]
    tq = q.shape[0]
    n_pairs = n_kv // 2

    @pl.when(pl.program_id(2) == 0)
    def _():
        kk = k_ref[...].astype(F32)
        k_sq = _dot_nt(jnp.ones((8, MLA_HQ), BF16), (kk * kk).astype(BF16))
        k_max = jnp.sqrt(jnp.max(k_sq, axis=1, keepdims=True))
        ref_smem[0] = qmax_ref[0, 0] * k_max[0, 0]

    ref = ref_smem[0]

    def k_tile(pair, t):
        off = (2 * pair + t) * tk
        return k_ref[pl.ds(off if isinstance(off, int) else pl.multiple_of(off, tk), tk), :]

    def finish():
        inv_l = 1.0 / acc_ref[MLA_V:MLA_V + 1, :]
        o_ref[...] = (acc_ref[0:MLA_V, :] * inv_l).T.astype(BF16)

    @pl.when(ref <= ATTN_REF_MAX)
    def _bounded():
        def scores_exp(pair, sw, l):
            for t in range(2):
                p = jnp.exp2(_dot_nt(k_tile(pair, t), q) - ref)
                p_ref[2 * sw + t] = p.astype(BF16)
                l = l + jnp.sum(p, axis=0, keepdims=True)
            return l

        def accumulate(pair, sw):
            p2 = p_ref[2 * sw:2 * sw + 2].reshape(2 * tk, tq)
            acc_ref[0:MLA_V, :] += _dot(vt_ref[pair, 0:MLA_V, :], p2)

        acc_ref[...] = jnp.zeros_like(acc_ref)
        l = scores_exp(0, 0, jnp.zeros((1, tq), F32))
        for pair in range(1, n_pairs):
            l = scores_exp(pair, pair % 2, l)
            accumulate(pair - 1, (pair - 1) % 2)
        accumulate(n_pairs - 1, (n_pairs - 1) % 2)
        o_ref[...] = (acc_ref[0:MLA_V, :] * (1.0 / l)).T.astype(BF16)

    @pl.when(ref > ATTN_REF_MAX)
    def _online():
        def scores(pair, sw):
            for t in range(2):
                s = _dot_nt(k_tile(pair, t), q)
                s_ref[2 * sw + t] = s
                mx_ref[2 * sw + t] = jnp.max(s, axis=0, keepdims=True)

        def softmax(sw, m):
            for t in range(2):
                slot = 2 * sw + t
                m_new = jnp.maximum(m, mx_ref[slot])
                al_ref[slot] = jnp.exp2(m - m_new)
                p_ref[slot] = jnp.exp2(s_ref[slot] - m_new).astype(BF16)
                m = m_new
            return m

        def accumulate(pair, sw):
            for t in range(2):
                slot = 2 * sw + t
                acc_ref[...] = (al_ref[slot] * acc_ref[...]
                                + _dot(vt_ref[pair, :, t * tk:(t + 1) * tk], p_ref[slot]))

        acc_ref[...] = jnp.zeros_like(acc_ref)
        scores(0, 0)
        scores(1, 1)
        m = softmax(0, jnp.full((1, tq), -1e30, F32))

        def body(it, m):
            pair = 1 + 2 * it
            scores(pair + 1, 0)
            m = softmax(1, m)
            accumulate(pair - 1, 0)
            scores(pair + 2, 1)
            m = softmax(0, m)
            accumulate(pair, 1)
            return m

        m = lax.fori_loop(0, (n_pairs - 2) // 2, body, m)
        softmax(1, m)
        accumulate(n_pairs - 2, 0)
        accumulate(n_pairs - 1, 1)
        finish()


def _attn(qmax, q, k, vt, B, S, tq, tk):
    assert (S // tk) % 4 == 0, "kv tiles are consumed in pairs, two pairs per loop pass"
    return pl.pallas_call(
        functools.partial(_attn_kernel, tk=tk, n_kv=S // tk),
        grid=(B, MLA_HEADS, S // tq),
        in_specs=[pl.BlockSpec(memory_space=pltpu.SMEM),
                  pl.BlockSpec((None, None, tq, MLA_HQ), lambda b, h, i: (b, h, i, 0)),
                  pl.BlockSpec((None, None, S, MLA_HQ), lambda b, h, i: (b, h, 0, 0)),
                  pl.BlockSpec((None, None, S // (2 * tk), MLA_VX, 2 * tk), lambda b, h, i: (b, h, 0, 0, 0))],
        out_specs=pl.BlockSpec((None, tq, MLA_V), lambda b, h, i: (b, i, h)),
        out_shape=jax.ShapeDtypeStruct((B, S, MLA_HEADS * MLA_V), BF16),
        scratch_shapes=[pltpu.VMEM((MLA_VX, tq), F32), pltpu.VMEM((4, tk, tq), F32),
                        pltpu.VMEM((4, tk, tq), BF16), pltpu.VMEM((4, 1, tq), F32),
                        pltpu.VMEM((4, 1, tq), F32), pltpu.SMEM((1,), F32)],
        compiler_params=_params(("parallel", "parallel", "arbitrary")),
        name="mla_attn",
    )(qmax, q, k, vt)


def _post_mix_kernel(x_ref, of_ref, og_ref, r_ref, gn_ref, ob_ref, gate_ref, wga_ref, wmo_ref, wo_ref,
                     gf_ref, wrt_ref, x1_ref, h2_ref, aff_ref):
    heads = []
    for hd in range(GLA_HEADS):
        cols = slice(hd * GLA_DV, (hd + 1) * GLA_DV)
        tot = of_ref[:, cols].astype(F32) + og_ref[:, cols].astype(F32)
        gate = r_ref[:, cols].astype(F32)
        heads.append((_rms(tot, gn_ref[...]) * (gate * _sigmoid(gate))).astype(BF16))
    ya = _dot(jnp.concatenate(heads, axis=1), wga_ref[...])
    yb = _dot(ob_ref[...], wmo_ref[...])
    ga = gate_ref[:, 0:D_MODEL].astype(F32)
    gb = gate_ref[:, D_MODEL:2 * D_MODEL].astype(F32)
    merged = (ga * ya + gb * yb).astype(BF16)
    x1 = x_ref[...] + _dot(merged, wo_ref[...])
    x1_ref[...] = x1
    h2 = _rms(x1, gf_ref[...]).astype(BF16)
    h2_ref[...] = h2
    logits = _dot_nt(wrt_ref[...], h2)
    e = jnp.exp(logits - jnp.max(logits, axis=0, keepdims=True))
    aff_ref[...] = e / jnp.sum(e, axis=0, keepdims=True)


def _post_mix(x2, o_fwd, o_bwd, gr, gnorm, ob, gates, w_ga, w_mo, w_o, g_ffn, w_rt, B, S, tm):
    nsb = S // tm
    row = lambda n: pl.BlockSpec((tm, n), lambda i: (i, 0))
    return pl.pallas_call(
        _post_mix_kernel,
        grid=(B * nsb,),
        in_specs=[row(D_MODEL), row(D_MODEL), row(D_MODEL), row(D_MODEL), _const_spec(gnorm.shape),
                  row(D_MODEL), row(2 * D_MODEL), _const_spec(w_ga.shape), _const_spec(w_mo.shape), _const_spec(w_o.shape),
                  _const_spec(g_ffn.shape), _const_spec(w_rt.shape)],
        out_specs=[row(D_MODEL), row(D_MODEL),
                   pl.BlockSpec((None, N_EXPERTS, tm), lambda i: (i // nsb, 0, i % nsb))],
        out_shape=[jax.ShapeDtypeStruct((B * S, D_MODEL), F32),
                   jax.ShapeDtypeStruct((B * S, D_MODEL), BF16),
                   jax.ShapeDtypeStruct((B, N_EXPERTS, S), F32)],
        compiler_params=_params(("parallel",)),
        name="post_mix",
    )(x2, o_fwd, o_bwd, gr, gnorm, ob, gates, w_ga, w_mo, w_o, g_ffn, w_rt)


def _select_kernel(aff_ref, pos_ref, pre_ref, *, cap, pblk):
    aff = aff_ref[...]
    S = aff.shape[1]
    bits = pltpu.bitcast(aff, jnp.int32)

    def search(i, thr):
        cand = thr | jnp.left_shift(jnp.int32(1), 30 - i)
        cnt = jnp.sum(jnp.where(bits >= cand, 1.0, 0.0), axis=1, keepdims=True)
        return jnp.where(cnt >= cap, cand, thr)

    thr = lax.fori_loop(0, 31, search, jnp.zeros((aff.shape[0], 1), jnp.int32))
    gt = bits > thr
    eq = bits == thr
    need = cap - jnp.sum(jnp.where(gt, 1.0, 0.0), axis=1, keepdims=True)

    r = lax.broadcasted_iota(jnp.int32, (pblk, pblk), 0)
    c = lax.broadcasted_iota(jnp.int32, (pblk, pblk), 1)
    before = (r < c).astype(BF16)

    def prefix_blocks(mask_of_block):
        carry = jnp.zeros((aff.shape[0], 1), F32)
        out = []
        for j in range(S // pblk):
            mk = mask_of_block(j)
            out.append(_dot(mk.astype(BF16), before) + carry)
            carry = carry + jnp.sum(mk, axis=1, keepdims=True)
        return out

    blk = lambda a, j: a[:, j * pblk:(j + 1) * pblk]
    tie_pre = prefix_blocks(lambda j: jnp.where(blk(eq, j), 1.0, 0.0))
    sel = [jnp.where(blk(gt, j) | (blk(eq, j) & (tie_pre[j] < need)), 1.0, 0.0)
           for j in range(S // pblk)]
    sel_pre = prefix_blocks(lambda j: sel[j])
    for j in range(S // pblk):
        pre_i = sel_pre[j].astype(jnp.int32)
        pre_ref[:, j * pblk:(j + 1) * pblk] = pre_i
        pos_ref[:, j * pblk:(j + 1) * pblk] = jnp.where(sel[j] > 0.5, pre_i, -1)


def _select(aff_t, cap, pblk):
    B, E, S = aff_t.shape
    spec = pl.BlockSpec((None, E, S), lambda b: (b, 0, 0))
    return pl.pallas_call(
        functools.partial(_select_kernel, cap=cap, pblk=pblk),
        grid=(B,),
        in_specs=[spec],
        out_specs=[spec, spec],
        out_shape=[jax.ShapeDtypeStruct((B, E, S), jnp.int32)] * 2,
        compiler_params=_params(("parallel",)),
        name="ec_select",
    )(aff_t)


def _ec_ffn_kernel(bs_ref, pos_ref, val_ref, h2_ref, wg_ref, wu_ref, wd_ref, x1_hbm,
                   out_ref, sem, *, n_parts, n_sub, tp, grp):
    b, part, e = pl.program_id(0), pl.program_id(1), pl.program_id(2)

    @pl.when(e == 0)
    def _():
        row0 = pl.multiple_of((b * n_parts + part) * tp, tp)
        cp = pltpu.make_async_copy(x1_hbm.at[pl.ds(row0, tp), :], out_ref, sem)
        cp.start()
        cp.wait()

    ts = tp // n_sub
    base = (b * N_EXPERTS + e) * (n_parts * n_sub + 1) + part * n_sub
    slot_iota = lax.broadcasted_iota(jnp.int32, (grp, ts), 0)

    for k in range(n_sub):
        s_lo = bs_ref[base + k]
        s_hi = bs_ref[base + k + 1]
        tok = slice(k * ts, (k + 1) * ts)

        def group(g, _, s_lo=s_lo, tok=tok):
            hit = pos_ref[:, tok] == (slot_iota + (s_lo + g * grp))
            xe = _dot(jnp.where(hit, 1.0, 0.0).astype(BF16), h2_ref[tok, :]).astype(BF16)
            a = _dot(xe, wg_ref[...])
            u = _dot(xe, wu_ref[...])
            mid = (a * _sigmoid(a) * u).astype(BF16)
            ye = _dot(mid, wd_ref[...]).astype(BF16)
            w = jnp.where(hit, val_ref[:, tok], 0.0).astype(BF16)
            out_ref[tok, :] += _dot_tn(w, ye)
            return 0

        lax.fori_loop(0, (s_hi - s_lo + grp - 1) // grp, group, 0)


def _ec_ffn(bs, pos4, val4, h2, wg, wu, wd, x1, B, S, n_parts, n_sub, grp):
    tp = S // n_parts
    row_spec = pl.BlockSpec((None, None, 1, tp), lambda b, p, e, bs_: (b, e, 0, p))
    w_spec = lambda shp: pl.BlockSpec((None,) + shp, lambda b, p, e, bs_: (e, 0, 0))
    part_spec = pl.BlockSpec((tp, D_MODEL), lambda b, p, e, bs_: (b * n_parts + p, 0))
    grid_spec = pltpu.PrefetchScalarGridSpec(
        num_scalar_prefetch=1,
        grid=(B, n_parts, N_EXPERTS),
        in_specs=[row_spec, row_spec, part_spec,
                  w_spec((D_MODEL, EXPERT_FF)), w_spec((D_MODEL, EXPERT_FF)),
                  w_spec((EXPERT_FF, D_MODEL)), pl.BlockSpec(memory_space=pl.ANY)],
        out_specs=part_spec,
        scratch_shapes=[pltpu.SemaphoreType.DMA(())],
    )
    return pl.pallas_call(
        functools.partial(_ec_ffn_kernel, n_parts=n_parts, n_sub=n_sub, tp=tp, grp=grp),
        grid_spec=grid_spec,
        out_shape=jax.ShapeDtypeStruct((B * S, D_MODEL), F32),
        compiler_params=_params(("parallel", "parallel", "arbitrary")),
        name="ec_ffn",
    )(bs, pos4, val4, h2, wg, wu, wd, x1)


def _tiles(S):
    return dict(tm=min(256, S), tb=min(256, S), tq=min(1024, S), tk=min(512, S // 4),
                tmix=min(512, S), pblk=min(256, S),
                n_parts=max(1, min(4, S // 256)), n_sub=2, grp=160)


def kernel(x, positions, g_mix, w_in, gla_wa2_f, gla_ba_f, gla_wa2_b, gla_ba_b, gla_onorm,
           w_gla_out, mla_q_norm, w_uq, mla_kv_norm, w_ukv, q_head_norm, k_head_norm, w_mla_out,
           w_o, g_ffn, w_router, w_gate_e, w_up_e, w_down_e):
    B, S, D = x.shape
    assert D == D_MODEL and S % GLA_CHUNK == 0
    t = _tiles(S)
    T = B * S
    x2 = x.reshape(T, D)
    row = lambda v: v.reshape(1, -1).astype(F32)

    nqk, nv = GLA_HEADS * GLA_DK, GLA_HEADS * GLA_DV
    o_dec = 2 * nqk + 2 * nv
    o_mla = o_dec + 2 * GLA_GATE_RANK
    o_gate = o_mla + MLA_Q_RANK + MLA_KV_RANK + MLA_ROPE
    w_main = w_in[:, :o_dec].astype(BF16)
    w_dec = jnp.pad(w_in[:, o_dec:o_mla], ((0, 0), (0, LANES - 2 * GLA_GATE_RANK))).astype(BF16)
    w_mla = jnp.pad(w_in[:, o_mla:o_gate], ((0, 0), (0, MLA_ROPE))).astype(BF16)
    w_gates = w_in[:, o_gate:].astype(BF16)
    wa2 = jnp.zeros((LANES, 2 * nqk), F32)
    wa2 = wa2.at[0:GLA_GATE_RANK, 0:nqk].set(gla_wa2_f)
    wa2 = wa2.at[GLA_GATE_RANK:2 * GLA_GATE_RANK, nqk:].set(gla_wa2_b).astype(BF16)
    ba = jnp.concatenate([gla_ba_f, gla_ba_b]).reshape(1, -1)


    w_uq3 = w_uq.reshape(MLA_Q_RANK, MLA_HEADS, MLA_HQ)
    w_uq_r = jnp.concatenate([w_uq3[:, :, :MLA_NOPE].reshape(MLA_Q_RANK, -1),
                              w_uq3[:, :, MLA_NOPE:].reshape(MLA_Q_RANK, -1)], axis=1).astype(BF16)
    w_ukv3 = w_ukv.reshape(MLA_KV_RANK, MLA_HEADS, MLA_NOPE + MLA_V)
    w_uk = w_ukv3[:, :, :MLA_NOPE].reshape(MLA_KV_RANK, -1).astype(BF16)
    w_vt = w_ukv3[:, :, MLA_NOPE:].reshape(MLA_KV_RANK, -1).T.astype(BF16)
    half = MLA_ROPE // 2
    freqs = ROPE_THETA ** (-jnp.arange(half, dtype=F32) / half)
    freq128 = jnp.tile(freqs, LANES // half).reshape(1, LANES)
    sign128 = jnp.tile(jnp.concatenate([-jnp.ones(half, F32), jnp.ones(half, F32)]), 2).reshape(1, LANES)
    qhr = jnp.tile(q_head_norm[MLA_NOPE:], 2).reshape(1, LANES)
    khr = jnp.pad(k_head_norm[MLA_NOPE:], (0, MLA_ROPE)).reshape(1, LANES)
    q_scale = math.log2(math.e) / math.sqrt(MLA_HQ)
    mla_consts = (freq128, sign128, w_mla, row(mla_q_norm), w_uq_r, row(mla_kv_norm), w_uk, w_vt,
                  row(q_head_norm[:MLA_NOPE]), qhr, row(k_head_norm[:MLA_NOPE]), khr)
    gq, gk, gv, gr, la, gates, q_b, k_b, v_t = _in_proj(
        x2, row(g_mix), (w_main, w_dec, wa2, ba, w_gates), positions.reshape(T, 1).astype(F32), mla_consts,
        B, S, min(t["tm"], t["tk"]), 2 * t["tk"], q_scale)

    o_fwd, o_bwd = _gla(gq, gk, gv, la, B, S, t["tb"])
    q_bound = (jnp.max(jnp.abs(q_head_norm)) * (math.sqrt(MLA_HQ) * q_scale)).reshape(1, 1).astype(F32)
    o_b = _attn(q_bound, q_b, k_b, v_t, B, S, t["tq"], t["tk"]).reshape(T, MLA_HEADS * MLA_V)

    x1, h2, aff_t = _post_mix(x2, o_fwd, o_bwd, gr, row(gla_onorm), o_b, gates, w_gla_out.astype(BF16),
                              w_mla_out.astype(BF16), w_o.astype(BF16), row(g_ffn),
                              w_router.T.astype(BF16), B, S, t["tmix"])

    cap = EC_CAPACITY * S // N_EXPERTS
    pos, pre = _select(aff_t, cap, t["pblk"])
    sub_tokens = S // (t["n_parts"] * t["n_sub"])
    bs = jnp.concatenate([pre[:, :, ::sub_tokens], jnp.full((B, N_EXPERTS, 1), cap, jnp.int32)], axis=2)
    out = _ec_ffn(bs.reshape(-1), pos.reshape(B, N_EXPERTS, 1, S), aff_t.reshape(B, N_EXPERTS, 1, S),
                  h2, w_gate_e.astype(BF16), w_up_e.astype(BF16), w_down_e.astype(BF16), x1,
                  B, S, t["n_parts"], t["n_sub"], t["grp"])
    return out.reshape(B, S, D)
```

```python
import functools
import math

import jax
import jax.numpy as jnp
from jax import lax
from jax.experimental import pallas as pl
from jax.experimental.pallas import tpu as pltpu

D_MODEL = 1024
GLA_HEADS = 4
GLA_DK = 128
GLA_DV = 256
GLA_GATE_RANK = 16
GLA_TAU = 16.0
GLA_CHUNK = 64
MLA_HEADS = 8
MLA_NOPE = 128
MLA_ROPE = 64
MLA_V = 128
MLA_Q_RANK = 384
MLA_KV_RANK = 256
MLA_HQ = MLA_NOPE + MLA_ROPE
MLA_VX = MLA_V + 16
ROPE_THETA = 10000.0
N_EXPERTS = 16
EC_CAPACITY = 2
EXPERT_FF = 1024
EPS = 1e-6
NEG_LARGE = -1e30
F32_ORDER_BITS = 31

LANES = 128
VMEM_LIMIT = 52 * 1024 * 1024

BF16 = jnp.bfloat16
F32 = jnp.float32


def _dot(a, b):
    return jnp.dot(a, b, preferred_element_type=F32)


def _dot_nt(a, b):
    return lax.dot_general(a, b, (((1,), (1,)), ((), ())), preferred_element_type=F32)


def _dot_tn(a, b):
    return lax.dot_general(a, b, (((0,), (0,)), ((), ())), preferred_element_type=F32)


def _rms(x, gain):
    ms = jnp.mean(x * x, axis=-1, keepdims=True)
    return x * lax.rsqrt(ms + EPS) * gain


def _sigmoid(x):
    return 0.5 * jnp.tanh(0.5 * x) + 0.5


def _const_spec(shape):
    nd = len(shape)
    return pl.BlockSpec(shape, lambda *_: (0,) * nd, pipeline_mode=pl.Buffered(1))


def _params(sem):
    return pltpu.CompilerParams(dimension_semantics=sem, vmem_limit_bytes=VMEM_LIMIT)


def _gla_in_body(h, wmain_ref, wdec_ref, wa2_ref, ba_ref, wgate_ref,
                 q_ref, k_ref, v_ref, r_ref, la_ref, gate_ref):
    nqk = GLA_HEADS * GLA_DK
    nv = GLA_HEADS * GLA_DV
    dec = _dot(h, wdec_ref[...]).astype(BF16)
    q_ref[...] = (_dot(h, wmain_ref[:, 0:nqk]) * (GLA_DK ** -0.5)).astype(BF16)
    k_ref[...] = _dot(h, wmain_ref[:, nqk:2 * nqk]).astype(BF16)
    z = _dot(dec, wa2_ref[...]) + ba_ref[...]
    log_sig = jnp.minimum(z, 0.0) - jnp.log1p(jnp.exp(-jnp.abs(z)))
    la_ref[...] = log_sig * (1.0 / GLA_TAU)
    v_ref[...] = _dot(h, wmain_ref[:, 2 * nqk:2 * nqk + nv]).astype(BF16)
    r_ref[...] = _dot(h, wmain_ref[:, 2 * nqk + nv:2 * nqk + 2 * nv]).astype(BF16)
    gate_ref[...] = _sigmoid(_dot(h, wgate_ref[...])).astype(BF16)


def _rope128(x, cos, sin_signed):
    lane = lax.broadcasted_iota(jnp.int32, x.shape, 1)
    first_half = (lane % MLA_ROPE) < (MLA_ROPE // 2)
    partner = jnp.where(first_half,
                        pltpu.roll(x, LANES - MLA_ROPE // 2, 1),
                        pltpu.roll(x, MLA_ROPE // 2, 1))
    return x * cos + partner * sin_signed


def _mla_in_body(h, pos_ref, freq_ref, sign_ref, wmla_ref, qn_ref, wuq_ref,
                 kvn_ref, wuk_ref, wvt_ref, qhn_ref, qhr_ref, khn_ref, khr_ref,
                 q_ref, k_ref, vt_ref, *, q_scale):
    lat = _dot(h, wmla_ref[...])
    cq = lat[:, 0:MLA_Q_RANK]
    ckv = lat[:, MLA_Q_RANK:MLA_Q_RANK + MLA_KV_RANK]
    kr2 = lat[:, MLA_Q_RANK + MLA_KV_RANK:MLA_Q_RANK + MLA_KV_RANK + LANES]

    ang = pos_ref[...] * freq_ref[...]
    cos = jnp.cos(ang)
    sin_signed = jnp.sin(ang) * sign_ref[...]
    lane = lax.broadcasted_iota(jnp.int32, ang.shape, 1)
    low = lane < MLA_ROPE

    nn = MLA_HEADS * MLA_NOPE
    qf = _dot(_rms(cq, qn_ref[...]).astype(BF16), wuq_ref[...])
    ckvn = _rms(ckv, kvn_ref[...]).astype(BF16)
    kn = _dot(ckvn, wuk_ref[...])
    vt = _dot_nt(wvt_ref[...], ckvn).astype(BF16)
    pad_row = lax.broadcasted_iota(jnp.int32, (MLA_VX - MLA_V, vt.shape[1]), 0)
    ones_then_zeros = jnp.where(pad_row == 0, 1.0, 0.0).astype(BF16)
    for hd in range(MLA_HEADS):
        vt_ref[hd, 0:MLA_V, :] = vt[hd * MLA_V:(hd + 1) * MLA_V, :]
        vt_ref[hd, MLA_V:MLA_VX, :] = ones_then_zeros

    ss_kr = jnp.sum(kr2 * kr2, axis=-1, keepdims=True)
    kr_rot = _rope128(kr2 * khr_ref[...], cos, sin_signed)[:, 0:MLA_ROPE]

    for pair in range(MLA_HEADS // 2):
        xr = qf[:, nn + pair * LANES: nn + (pair + 1) * LANES]
        sq = xr * xr
        ss_lo = jnp.sum(jnp.where(low, sq, 0.0), axis=-1, keepdims=True)
        ss_hi = jnp.sum(sq, axis=-1, keepdims=True) - ss_lo
        rstd = []
        for j, ss_r in enumerate((ss_lo, ss_hi)):
            hd = 2 * pair + j
            xn = qf[:, hd * MLA_NOPE:(hd + 1) * MLA_NOPE]
            ss = jnp.sum(xn * xn, axis=-1, keepdims=True) + ss_r
            r = lax.rsqrt(ss * (1.0 / MLA_HQ) + EPS)
            rstd.append(r)
            q_ref[hd, :, 0:MLA_NOPE] = (xn * r * (qhn_ref[...] * q_scale)).astype(BF16)
        scaled = xr * jnp.where(low, rstd[0], rstd[1]) * (qhr_ref[...] * q_scale)
        rot = _rope128(scaled, cos, sin_signed).astype(BF16)
        q_ref[2 * pair, :, MLA_NOPE:MLA_HQ] = rot[:, 0:MLA_ROPE]
        q_ref[2 * pair + 1, :, MLA_NOPE:MLA_HQ] = rot[:, MLA_ROPE:LANES]

    for hd in range(MLA_HEADS):
        xn = kn[:, hd * MLA_NOPE:(hd + 1) * MLA_NOPE]
        ss = jnp.sum(xn * xn, axis=-1, keepdims=True) + ss_kr
        r = lax.rsqrt(ss * (1.0 / MLA_HQ) + EPS)
        k_ref[hd, :, 0:MLA_NOPE] = (xn * r * khn_ref[...]).astype(BF16)
        k_ref[hd, :, MLA_NOPE:MLA_HQ] = (kr_rot * r).astype(BF16)


N_GLA_IN, N_GLA_OUT, N_MLA_IN = 5, 6, 13


def _in_proj_kernel(x_ref, g_ref, *refs, q_scale):
    h = _rms(x_ref[...], g_ref[...]).astype(BF16)
    gla_in = refs[:N_GLA_IN]
    mla_in = refs[N_GLA_IN:N_GLA_IN + N_MLA_IN]
    gla_out = refs[N_GLA_IN + N_MLA_IN:N_GLA_IN + N_MLA_IN + N_GLA_OUT]
    mla_out = refs[N_GLA_IN + N_MLA_IN + N_GLA_OUT:]
    _mla_in_body(h, *mla_in, *mla_out, q_scale=q_scale)
    _gla_in_body(h, *gla_in, *gla_out)


def _in_proj(x2, g_mix, gla_consts, pos, mla_consts, B, S, tm, tk, q_scale):
    T = B * S
    nsb = S // tm
    per_kv = tk // tm
    assert tk % tm == 0 and len(gla_consts) == N_GLA_IN and len(mla_consts) == N_MLA_IN - 1
    nqk = GLA_HEADS * GLA_DK
    nv = GLA_HEADS * GLA_DV
    row = lambda n: pl.BlockSpec((tm, n), lambda i: (i, 0))
    head_spec = pl.BlockSpec((None, MLA_HEADS, tm, MLA_HQ), lambda i: (i // nsb, 0, i % nsb, 0))
    in_specs = ([row(D_MODEL), _const_spec(g_mix.shape)] + [_const_spec(c.shape) for c in gla_consts]
                + [row(1)] + [_const_spec(c.shape) for c in mla_consts])
    return pl.pallas_call(
        functools.partial(_in_proj_kernel, q_scale=q_scale),
        grid=(B * nsb,),
        in_specs=in_specs,
        out_specs=[row(nqk), row(nqk), row(nv), row(nv), row(2 * nqk), row(2 * D_MODEL),
                   head_spec, head_spec,
                   pl.BlockSpec((None, MLA_HEADS, None, MLA_VX, tm),
                                lambda i: (i // nsb, 0, (i % nsb) // per_kv, 0, (i % nsb) % per_kv))],
        out_shape=[jax.ShapeDtypeStruct((T, nqk), BF16), jax.ShapeDtypeStruct((T, nqk), BF16),
                   jax.ShapeDtypeStruct((T, nv), BF16), jax.ShapeDtypeStruct((T, nv), BF16),
                   jax.ShapeDtypeStruct((T, 2 * nqk), F32), jax.ShapeDtypeStruct((T, 2 * D_MODEL), BF16),
                   jax.ShapeDtypeStruct((B, MLA_HEADS, S, MLA_HQ), BF16),
                   jax.ShapeDtypeStruct((B, MLA_HEADS, S, MLA_HQ), BF16),
                   jax.ShapeDtypeStruct((B, MLA_HEADS, S // tk, MLA_VX, tk), BF16)],
        compiler_params=_params(("parallel",)),
        name="in_proj",
    )(x2, g_mix, *gla_consts, pos, *mla_consts)


def _gla_block(q_ref, k_ref, v_ref, la_ref, st_ref, emit, *, reverse, n_chunks):
    C = GLA_CHUNK
    row = lax.broadcasted_iota(jnp.int32, (C, C), 0)
    col = lax.broadcasted_iota(jnp.int32, (C, C), 1)
    if reverse:
        cum_mat = (col >= row).astype(BF16)
        att_mask = col > row
        last = 0
    else:
        cum_mat = (col <= row).astype(BF16)
        att_mask = col <= row
        last = C - 1
    order = range(n_chunks - 1, -1, -1) if reverse else range(n_chunks)
    log_decay = {}
    for c in order:
        la = la_ref[c * C:(c + 1) * C, :]
        la_hi = la.astype(BF16)
        la_lo = (la - la_hi.astype(F32)).astype(BF16)
        log_decay[c] = _dot(cum_mat, la_hi) + _dot(cum_mat, la_lo)
    for hd in range(GLA_HEADS):
        kcols = slice(hd * GLA_DK, (hd + 1) * GLA_DK)
        vcols = slice(hd * GLA_DV, (hd + 1) * GLA_DV)
        for c in order:
            rows = slice(c * C, (c + 1) * C)
            b = log_decay[c][:, kcols]
            b_last = b[last:last + 1, :]
            q = q_ref[rows, kcols].astype(F32)
            k = k_ref[rows, kcols].astype(F32)
            v = v_ref[rows, vcols]
            q_t = (q * jnp.exp(b)).astype(BF16)
            k_t = (k * jnp.exp(-b)).astype(BF16)
            k_dec = (k * jnp.exp(b_last - b)).astype(BF16)
            att = jnp.where(att_mask, _dot_nt(q_t, k_t), 0.0).astype(BF16)
            st = st_ref[hd]
            o = _dot(att, v) + _dot_nt(q_t, st.astype(BF16))
            st_ref[hd] = st * jnp.exp(b_last) + _dot_tn(v, k_dec)
            emit(rows, vcols, o)


def _gla_kernel(qf_ref, kf_ref, vf_ref, laf_ref, qb_ref, kb_ref, vb_ref, lab_ref, of_ref, ob_ref,
                stf_ref, stb_ref, *, n_chunks):
    @pl.when(pl.program_id(1) == 0)
    def _():
        stf_ref[...] = jnp.zeros_like(stf_ref)
        stb_ref[...] = jnp.zeros_like(stb_ref)

    def emit_f(rows, vcols, o):
        of_ref[rows, vcols] = o.astype(BF16)

    def emit_b(rows, vcols, o):
        ob_ref[rows, vcols] = o.astype(BF16)

    _gla_block(qf_ref, kf_ref, vf_ref, laf_ref, stf_ref, emit_f, reverse=False, n_chunks=n_chunks)
    _gla_block(qb_ref, kb_ref, vb_ref, lab_ref, stb_ref, emit_b, reverse=True, n_chunks=n_chunks)


def _gla(gq, gk, gv, la, B, S, tb):
    nb = S // tb
    nqk = GLA_HEADS * GLA_DK
    nv = GLA_HEADS * GLA_DV
    state = pltpu.VMEM((GLA_HEADS, GLA_DV, GLA_DK), F32)
    fwd = lambda n, c=0: pl.BlockSpec((tb, n), lambda b, i: (b * nb + i, c))
    rev = lambda n, c=0: pl.BlockSpec((tb, n), lambda b, i: (b * nb + nb - 1 - i, c))
    return pl.pallas_call(
        functools.partial(_gla_kernel, n_chunks=tb // GLA_CHUNK),
        grid=(B, nb),
        in_specs=[fwd(nqk), fwd(nqk), fwd(nv), fwd(nqk, 0), rev(nqk), rev(nqk), rev(nv), rev(nqk, 1)],
        out_specs=[fwd(nv), rev(nv)],
        out_shape=[jax.ShapeDtypeStruct((B * S, nv), BF16)] * 2,
        scratch_shapes=[state, state],
        compiler_params=_params(("parallel", "arbitrary")),
        name="gla_scan",
    )(gq, gk, gv, la, gq, gk, gv, la)


ATTN_REF_MAX = 48.0


def _attn_kernel(qmax_ref, q_ref, k_ref, vt_ref, o_ref, acc_ref, s_ref, p_ref, mx_ref, al_ref, ref_smem,
                 *, tk, n_kv):
    q = q_ref[...]
    tq = q.shape[0]
    n_pairs = n_kv // 2

    @pl.when(pl.program_id(2) == 0)
    def _():
        kk = k_ref[...].astype(F32)
        k_sq = _dot_nt(jnp.ones((8, MLA_HQ), BF16), (kk * kk).astype(BF16))
        k_max = jnp.sqrt(jnp.max(k_sq, axis=1, keepdims=True))
        ref_smem[0] = qmax_ref[0, 0] * k_max[0, 0]

    ref = ref_smem[0]

    def k_tile(pair, t):
        off = (2 * pair + t) * tk
        return k_ref[pl.ds(off if isinstance(off, int) else pl.multiple_of(off, tk), tk), :]

    def finish():
        inv_l = 1.0 / acc_ref[MLA_V:MLA_V + 1, :]
        o_ref[...] = (acc_ref[0:MLA_V, :] * inv_l).T.astype(BF16)

    @pl.when(ref <= ATTN_REF_MAX)
    def _bounded():
        def scores_exp(pair, sw, l):
            for t in range(2):
                p = jnp.exp2(_dot_nt(k_tile(pair, t), q) - ref)
                p_ref[2 * sw + t] = p.astype(BF16)
                l = l + jnp.sum(p, axis=0, keepdims=True)
            return l

        def accumulate(pair, sw):
            p2 = p_ref[2 * sw:2 * sw + 2].reshape(2 * tk, tq)
            acc_ref[0:MLA_V, :] += _dot(vt_ref[pair, 0:MLA_V, :], p2)

        acc_ref[...] = jnp.zeros_like(acc_ref)
        l = scores_exp(0, 0, jnp.zeros((1, tq), F32))
        for pair in range(1, n_pairs):
            l = scores_exp(pair, pair % 2, l)
            accumulate(pair - 1, (pair - 1) % 2)
        accumulate(n_pairs - 1, (n_pairs - 1) % 2)
        o_ref[...] = (acc_ref[0:MLA_V, :] * (1.0 / l)).T.astype(BF16)

    @pl.when(ref > ATTN_REF_MAX)
    def _online():
        def scores(pair, sw):
            for t in range(2):
                s = _dot_nt(k_tile(pair, t), q)
                s_ref[2 * sw + t] = s
                mx_ref[2 * sw + t] = jnp.max(s, axis=0, keepdims=True)

        def softmax(sw, m):
            for t in range(2):
                slot = 2 * sw + t
                m_new = jnp.maximum(m, mx_ref[slot])
                al_ref[slot] = jnp.exp2(m - m_new)
                p_ref[slot] = jnp.exp2(s_ref[slot] - m_new).astype(BF16)
                m = m_new
            return m

        def accumulate(pair, sw):
            for t in range(2):
                slot = 2 * sw + t
                acc_ref[...] = (al_ref[slot] * acc_ref[...]
                                + _dot(vt_ref[pair, :, t * tk:(t + 1) * tk], p_ref[slot]))

        acc_ref[...] = jnp.zeros_like(acc_ref)
        scores(0, 0)
        scores(1, 1)
        m = softmax(0, jnp.full((1, tq), NEG_LARGE, F32))

        def body(it, m):
            pair = 1 + 2 * it
            scores(pair + 1, 0)
            m = softmax(1, m)
            accumulate(pair - 1, 0)
            scores(pair + 2, 1)
            m = softmax(0, m)
            accumulate(pair, 1)
            return m

        m = lax.fori_loop(0, (n_pairs - 2) // 2, body, m)
        softmax(1, m)
        accumulate(n_pairs - 2, 0)
        accumulate(n_pairs - 1, 1)
        finish()


def _attn(qmax, q, k, vt, B, S, tq, tk):
    assert (S // tk) % 4 == 0, "kv tiles are consumed in pairs, two pairs per loop pass"
    return pl.pallas_call(
        functools.partial(_attn_kernel, tk=tk, n_kv=S // tk),
        grid=(B, MLA_HEADS, S // tq),
        in_specs=[pl.BlockSpec(memory_space=pltpu.SMEM),
                  pl.BlockSpec((None, None, tq, MLA_HQ), lambda b, h, i: (b, h, i, 0)),
                  pl.BlockSpec((None, None, S, MLA_HQ), lambda b, h, i: (b, h, 0, 0)),
                  pl.BlockSpec((None, None, S // (2 * tk), MLA_VX, 2 * tk), lambda b, h, i: (b, h, 0, 0, 0))],
        out_specs=pl.BlockSpec((None, tq, MLA_V), lambda b, h, i: (b, i, h)),
        out_shape=jax.ShapeDtypeStruct((B, S, MLA_HEADS * MLA_V), BF16),
        scratch_shapes=[pltpu.VMEM((MLA_VX, tq), F32), pltpu.VMEM((4, tk, tq), F32),
                        pltpu.VMEM((4, tk, tq), BF16), pltpu.VMEM((4, 1, tq), F32),
                        pltpu.VMEM((4, 1, tq), F32), pltpu.SMEM((1,), F32)],
        compiler_params=_params(("parallel", "parallel", "arbitrary")),
        name="mla_attn",
    )(qmax, q, k, vt)


def _post_mix_kernel(x_ref, of_ref, og_ref, r_ref, gn_ref, ob_ref, gate_ref, wga_ref, wmo_ref, wo_ref,
                     gf_ref, wrt_ref, x1_ref, h2_ref, aff_ref):
    yb = _dot(ob_ref[...], wmo_ref[...])
    heads = []
    for hd in range(GLA_HEADS):
        cols = slice(hd * GLA_DV, (hd + 1) * GLA_DV)
        tot = of_ref[:, cols].astype(F32) + og_ref[:, cols].astype(F32)
        gate = r_ref[:, cols].astype(F32)
        heads.append((_rms(tot, gn_ref[...]) * (gate * _sigmoid(gate))).astype(BF16))
    ya = _dot(jnp.concatenate(heads, axis=1), wga_ref[...])
    ga = gate_ref[:, 0:D_MODEL].astype(F32)
    gb = gate_ref[:, D_MODEL:2 * D_MODEL].astype(F32)
    merged = (ga * ya + gb * yb).astype(BF16)
    x1 = x_ref[...] + _dot(merged, wo_ref[...])
    x1_ref[...] = x1
    h2 = _rms(x1, gf_ref[...]).astype(BF16)
    h2_ref[...] = h2
    logits = _dot_nt(wrt_ref[...], h2)
    e = jnp.exp(logits - jnp.max(logits, axis=0, keepdims=True))
    aff_ref[...] = e / jnp.sum(e, axis=0, keepdims=True)


def _post_mix(x2, o_fwd, o_bwd, gr, gnorm, ob, gates, w_ga, w_mo, w_o, g_ffn, w_rt, B, S, tm):
    nsb = S // tm
    row = lambda n: pl.BlockSpec((tm, n), lambda i: (i, 0))
    return pl.pallas_call(
        _post_mix_kernel,
        grid=(B * nsb,),
        in_specs=[row(D_MODEL), row(D_MODEL), row(D_MODEL), row(D_MODEL), _const_spec(gnorm.shape),
                  row(D_MODEL), row(2 * D_MODEL), _const_spec(w_ga.shape), _const_spec(w_mo.shape), _const_spec(w_o.shape),
                  _const_spec(g_ffn.shape), _const_spec(w_rt.shape)],
        out_specs=[row(D_MODEL), row(D_MODEL),
                   pl.BlockSpec((None, N_EXPERTS, tm), lambda i: (i // nsb, 0, i % nsb))],
        out_shape=[jax.ShapeDtypeStruct((B * S, D_MODEL), F32),
                   jax.ShapeDtypeStruct((B * S, D_MODEL), BF16),
                   jax.ShapeDtypeStruct((B, N_EXPERTS, S), F32)],
        compiler_params=_params(("parallel",)),
        name="post_mix",
    )(x2, o_fwd, o_bwd, gr, gnorm, ob, gates, w_ga, w_mo, w_o, g_ffn, w_rt)


def _select_kernel(aff_ref, pos_ref, pre_ref, *, cap, pblk):
    aff = aff_ref[...]
    S = aff.shape[1]
    bits = pltpu.bitcast(aff, jnp.int32)

    def search(i, thr):
        cand = thr | jnp.left_shift(jnp.int32(1), F32_ORDER_BITS - 1 - i)
        cnt = jnp.sum(jnp.where(bits >= cand, 1.0, 0.0), axis=1, keepdims=True)
        return jnp.where(cnt >= cap, cand, thr)

    thr = lax.fori_loop(0, F32_ORDER_BITS, search, jnp.zeros((aff.shape[0], 1), jnp.int32))
    gt = bits > thr
    eq = bits == thr
    need = cap - jnp.sum(jnp.where(gt, 1.0, 0.0), axis=1, keepdims=True)

    r = lax.broadcasted_iota(jnp.int32, (pblk, pblk), 0)
    c = lax.broadcasted_iota(jnp.int32, (pblk, pblk), 1)
    before = (r < c).astype(BF16)

    def prefix_blocks(mask_of_block):
        carry = jnp.zeros((aff.shape[0], 1), F32)
        out = []
        for j in range(S // pblk):
            mk = mask_of_block(j)
            out.append(_dot(mk.astype(BF16), before) + carry)
            carry = carry + jnp.sum(mk, axis=1, keepdims=True)
        return out

    blk = lambda a, j: a[:, j * pblk:(j + 1) * pblk]
    tie_pre = prefix_blocks(lambda j: jnp.where(blk(eq, j), 1.0, 0.0))
    sel = [jnp.where(blk(gt, j) | (blk(eq, j) & (tie_pre[j] < need)), 1.0, 0.0)
           for j in range(S // pblk)]
    sel_pre = prefix_blocks(lambda j: sel[j])
    for j in range(S // pblk):
        pre_i = sel_pre[j].astype(jnp.int32)
        pre_ref[:, j * pblk:(j + 1) * pblk] = pre_i
        pos_ref[:, j * pblk:(j + 1) * pblk] = jnp.where(sel[j] > 0.5, pre_i, -1)


def _select(aff_t, cap, pblk):
    B, E, S = aff_t.shape
    spec = pl.BlockSpec((None, E, S), lambda b: (b, 0, 0))
    return pl.pallas_call(
        functools.partial(_select_kernel, cap=cap, pblk=pblk),
        grid=(B,),
        in_specs=[spec],
        out_specs=[spec, spec],
        out_shape=[jax.ShapeDtypeStruct((B, E, S), jnp.int32)] * 2,
        compiler_params=_params(("parallel",)),
        name="ec_select",
    )(aff_t)


def _ec_ffn_kernel(bs_ref, pos_ref, val_ref, h2_ref, wg_ref, wu_ref, wd_ref, x1_hbm,
                   out_ref, sem, *, n_parts, n_sub, tp, grp):
    b, part, e = pl.program_id(0), pl.program_id(1), pl.program_id(2)

    @pl.when(e == 0)
    def _():
        row0 = pl.multiple_of((b * n_parts + part) * tp, tp)
        cp = pltpu.make_async_copy(x1_hbm.at[pl.ds(row0, tp), :], out_ref, sem)
        cp.start()
        cp.wait()

    ts = tp // n_sub
    base = (b * N_EXPERTS + e) * (n_parts * n_sub + 1) + part * n_sub
    slot_iota = lax.broadcasted_iota(jnp.int32, (grp, ts), 0)

    for k in range(n_sub):
        s_lo = bs_ref[base + k]
        s_hi = bs_ref[base + k + 1]
        tok = slice(k * ts, (k + 1) * ts)

        def group(g, _, s_lo=s_lo, tok=tok):
            hit = pos_ref[:, tok] == (slot_iota + (s_lo + g * grp))
            xe = _dot(jnp.where(hit, 1.0, 0.0).astype(BF16), h2_ref[tok, :]).astype(BF16)
            a = _dot(xe, wg_ref[...])
            u = _dot(xe, wu_ref[...])
            mid = (a * _sigmoid(a) * u).astype(BF16)
            ye = _dot(mid, wd_ref[...]).astype(BF16)
            w = jnp.where(hit, val_ref[:, tok], 0.0).astype(BF16)
            out_ref[tok, :] += _dot_tn(w, ye)
            return 0

        lax.fori_loop(0, (s_hi - s_lo + grp - 1) // grp, group, 0)


def _ec_ffn(bs, pos4, val4, h2, wg, wu, wd, x1, B, S, n_parts, n_sub, grp):
    tp = S // n_parts
    row_spec = pl.BlockSpec((None, None, 1, tp), lambda b, p, e, bs_: (b, e, 0, p))
    w_spec = lambda shp: pl.BlockSpec((None,) + shp, lambda b, p, e, bs_: (e, 0, 0))
    part_spec = pl.BlockSpec((tp, D_MODEL), lambda b, p, e, bs_: (b * n_parts + p, 0))
    grid_spec = pltpu.PrefetchScalarGridSpec(
        num_scalar_prefetch=1,
        grid=(B, n_parts, N_EXPERTS),
        in_specs=[row_spec, row_spec, part_spec,
                  w_spec((D_MODEL, EXPERT_FF)), w_spec((D_MODEL, EXPERT_FF)),
                  w_spec((EXPERT_FF, D_MODEL)), pl.BlockSpec(memory_space=pl.ANY)],
        out_specs=part_spec,
        scratch_shapes=[pltpu.SemaphoreType.DMA(())],
    )
    return pl.pallas_call(
        functools.partial(_ec_ffn_kernel, n_parts=n_parts, n_sub=n_sub, tp=tp, grp=grp),
        grid_spec=grid_spec,
        out_shape=jax.ShapeDtypeStruct((B * S, D_MODEL), F32),
        compiler_params=_params(("parallel", "parallel", "arbitrary")),
        name="ec_ffn",
    )(bs, pos4, val4, h2, wg, wu, wd, x1)


def _tiles(S):
    return dict(tm=min(512, S), tb=min(512, S), tq=min(1024, S), tk=min(512, S // 4),
                tmix=min(512, S), pblk=min(256, S),
                n_parts=max(1, min(4, S // 256)), n_sub=2, grp=160)


def kernel(x, positions, g_mix, w_in, gla_wa2_f, gla_ba_f, gla_wa2_b, gla_ba_b, gla_onorm,
           w_gla_out, mla_q_norm, w_uq, mla_kv_norm, w_ukv, q_head_norm, k_head_norm, w_mla_out,
           w_o, g_ffn, w_router, w_gate_e, w_up_e, w_down_e):
    B, S, D = x.shape
    assert D == D_MODEL and S % GLA_CHUNK == 0
    t = _tiles(S)
    T = B * S
    x2 = x.reshape(T, D)
    row = lambda v: v.reshape(1, -1).astype(F32)

    nqk, nv = GLA_HEADS * GLA_DK, GLA_HEADS * GLA_DV
    o_dec = 2 * nqk + 2 * nv
    o_mla = o_dec + 2 * GLA_GATE_RANK
    o_gate = o_mla + MLA_Q_RANK + MLA_KV_RANK + MLA_ROPE
    w_main = w_in[:, :o_dec].astype(BF16)
    w_dec = jnp.pad(w_in[:, o_dec:o_mla], ((0, 0), (0, LANES - 2 * GLA_GATE_RANK))).astype(BF16)
    w_mla = jnp.pad(w_in[:, o_mla:o_gate], ((0, 0), (0, MLA_ROPE))).astype(BF16)
    w_gates = w_in[:, o_gate:].astype(BF16)
    wa2 = jnp.zeros((LANES, 2 * nqk), F32)
    wa2 = wa2.at[0:GLA_GATE_RANK, 0:nqk].set(gla_wa2_f)
    wa2 = wa2.at[GLA_GATE_RANK:2 * GLA_GATE_RANK, nqk:].set(gla_wa2_b).astype(BF16)
    ba = jnp.concatenate([gla_ba_f, gla_ba_b]).reshape(1, -1)


    w_uq3 = w_uq.reshape(MLA_Q_RANK, MLA_HEADS, MLA_HQ)
    w_uq_r = jnp.concatenate([w_uq3[:, :, :MLA_NOPE].reshape(MLA_Q_RANK, -1),
                              w_uq3[:, :, MLA_NOPE:].reshape(MLA_Q_RANK, -1)], axis=1).astype(BF16)
    w_ukv3 = w_ukv.reshape(MLA_KV_RANK, MLA_HEADS, MLA_NOPE + MLA_V)
    w_uk = w_ukv3[:, :, :MLA_NOPE].reshape(MLA_KV_RANK, -1).astype(BF16)
    w_vt = w_ukv3[:, :, MLA_NOPE:].reshape(MLA_KV_RANK, -1).T.astype(BF16)
    half = MLA_ROPE // 2
    freqs = ROPE_THETA ** (-jnp.arange(half, dtype=F32) / half)
    freq128 = jnp.tile(freqs, LANES // half).reshape(1, LANES)
    sign128 = jnp.tile(jnp.concatenate([-jnp.ones(half, F32), jnp.ones(half, F32)]), 2).reshape(1, LANES)
    qhr = jnp.tile(q_head_norm[MLA_NOPE:], 2).reshape(1, LANES)
    khr = jnp.pad(k_head_norm[MLA_NOPE:], (0, MLA_ROPE)).reshape(1, LANES)
    q_scale = math.log2(math.e) / math.sqrt(MLA_HQ)
    mla_consts = (freq128, sign128, w_mla, row(mla_q_norm), w_uq_r, row(mla_kv_norm), w_uk, w_vt,
                  row(q_head_norm[:MLA_NOPE]), qhr, row(k_head_norm[:MLA_NOPE]), khr)
    gq, gk, gv, gr, la, gates, q_b, k_b, v_t = _in_proj(
        x2, row(g_mix), (w_main, w_dec, wa2, ba, w_gates), positions.reshape(T, 1).astype(F32), mla_consts,
        B, S, min(t["tm"], t["tk"]), 2 * t["tk"], q_scale)

    o_fwd, o_bwd = _gla(gq, gk, gv, la, B, S, t["tb"])
    q_bound = (jnp.max(jnp.abs(q_head_norm)) * (math.sqrt(MLA_HQ) * q_scale)).reshape(1, 1).astype(F32)
    o_b = _attn(q_bound, q_b, k_b, v_t, B, S, t["tq"], t["tk"]).reshape(T, MLA_HEADS * MLA_V)

    x1, h2, aff_t = _post_mix(x2, o_fwd, o_bwd, gr, row(gla_onorm), o_b, gates, w_gla_out.astype(BF16),
                              w_mla_out.astype(BF16), w_o.astype(BF16), row(g_ffn),
                              w_router.T.astype(BF16), B, S, t["tmix"])

    cap = EC_CAPACITY * S // N_EXPERTS
    pos, pre = _select(aff_t, cap, t["pblk"])
    sub_tokens = S // (t["n_parts"] * t["n_sub"])
    bs = jnp.concatenate([pre[:, :, ::sub_tokens], jnp.full((B, N_EXPERTS, 1), cap, jnp.int32)], axis=2)
    out = _ec_ffn(bs.reshape(-1), pos.reshape(B, N_EXPERTS, 1, S), aff_t.reshape(B, N_EXPERTS, 1, S),
                  h2, w_gate_e.astype(BF16), w_up_e.astype(BF16), w_down_e.astype(BF16), x1,
                  B, S, t["n_parts"], t["n_sub"], t["grp"])
    return out.reshape(B, S, D)
```
